```python
import jax, jax.numpy as jnp
from jax import lax
import numpy as np

D_MODEL = 4096
BATCH = 8
SEQ = 2048
DEPTH = 2

CHUNK = 64
Q_BLOCK = 128
PLE_DIM = 256
D_FF = 8192
EPS = 1e-6
ROPE_THETA = 10000.0
N_BRANCH = 4
HEAD_DIM = 128
BRANCH_WIDTH = 1024
FOX_HEADS = 8
MLA_HEADS = 8
MLA_Q_LORA = 768
MLA_KV_LORA = 512
MLA_NOPE = 128
MLA_ROPE = 64
MLA_V = 128
CH_HEADS = 8
CH_PREV = 8
MAX_REL = 128
DSA_HEADS = 8
IDX_HEADS = 16
IDX_DIM = 64
TOPK_MAX = 256

SPLIT_SIZES = (
    FOX_HEADS * HEAD_DIM, FOX_HEADS * HEAD_DIM, FOX_HEADS * HEAD_DIM, FOX_HEADS,
    MLA_Q_LORA, MLA_KV_LORA, MLA_ROPE,
    CH_HEADS * HEAD_DIM, CH_HEADS * HEAD_DIM, CH_HEADS * HEAD_DIM,
    DSA_HEADS * HEAD_DIM, HEAD_DIM, HEAD_DIM, IDX_HEADS * IDX_DIM, IDX_DIM, IDX_HEADS,
)
IN_COLS = sum(SPLIT_SIZES)

kernel_name = "hybrid_chunk_causal_gated_mixers"

F32 = jnp.float32


def rmsnorm(x, g):
    x32 = x.astype(F32)
    y = x32 * lax.rsqrt(jnp.mean(x32 * x32, axis=-1, keepdims=True) + EPS)
    return (y * g.astype(F32)).astype(x.dtype)


def swiglu(x, wg, wu, wd):
    return (jax.nn.silu(x @ wg) * (x @ wu)) @ wd


def rope(x, pos):
    half = x.shape[-1] // 2
    inv = ROPE_THETA ** (-jnp.arange(half, dtype=F32) / half)
    ang = pos.astype(F32)[:, None] * inv[None, :]
    cos = jnp.cos(ang)[:, None, :]
    sin = jnp.sin(ang)[:, None, :]
    x1 = x[..., :half].astype(F32)
    x2 = x[..., half:].astype(F32)
    return jnp.concatenate([x1 * cos - x2 * sin, x1 * sin + x2 * cos], axis=-1).astype(x.dtype)


def prefix_block_attention(q, k, v, scale, frame_causal, decay=None):
    S = q.shape[1]
    outs = []
    for blk in range(S // Q_BLOCK):
        q0, q1 = blk * Q_BLOCK, (blk + 1) * Q_BLOCK
        s = jnp.einsum('bqhd,bkhd->bhqk', q[:, q0:q1], k[:, :q1]).astype(F32) * scale
        tq = jnp.arange(q0, q1)[:, None]
        tk = jnp.arange(q1)[None, :]
        allowed = (tk <= tq) if frame_causal else ((tk // CHUNK) <= (tq // CHUNK))
        if decay is not None:
            s = s + decay[:, :, q0:q1, None] - decay[:, :, None, :q1]
        s = jnp.where(allowed, s, -jnp.inf)
        pr = jax.nn.softmax(s, axis=-1).astype(v.dtype)
        outs.append(jnp.einsum('bhqk,bkhd->bqhd', pr, v[:, :q1]))
    return jnp.concatenate(outs, axis=1)


def chunk_band_attention(q, k, v, rel_table):
    B, S, H, d = q.shape
    nc = S // CHUNK
    nband = CH_PREV + 1
    qc = q.reshape(B, nc, CHUNK, H, d)
    pad = ((0, 0), (CH_PREV, 0), (0, 0), (0, 0), (0, 0))
    kp = jnp.pad(k.reshape(B, nc, CHUNK, H, d), pad)
    vp = jnp.pad(v.reshape(B, nc, CHUNK, H, d), pad)
    band_idx = jnp.arange(nc)[:, None] + jnp.arange(nband)[None, :]
    kb = kp[:, band_idx].reshape(B, nc, nband * CHUNK, H, d)
    vb = vp[:, band_idx].reshape(B, nc, nband * CHUNK, H, d)
    s = jnp.einsum('bcqhd,bckhd->bhcqk', qc, kb).astype(F32) * d ** -0.5
    qpos = jnp.arange(CHUNK)
    kpos = ((jnp.arange(nband) - CH_PREV)[:, None] * CHUNK + jnp.arange(CHUNK)[None, :]).reshape(-1)
    rel = jnp.clip(qpos[:, None] - kpos[None, :], -MAX_REL, MAX_REL) + MAX_REL
    bias = rel_table.astype(F32)[:, rel]
    key_chunk = jnp.arange(nc)[:, None] - CH_PREV + jnp.arange(nband)[None, :]
    kvalid = jnp.repeat(key_chunk >= 0, CHUNK, axis=1)
    s = jnp.where(kvalid[None, None, :, None, :], s + bias[None, :, None], -jnp.inf)
    pr = jax.nn.softmax(s, axis=-1).astype(v.dtype)
    o = jnp.einsum('bhcqk,bckhd->bcqhd', pr, vb)
    return o.reshape(B, S, H * d)


def dsa_attention(q, k, v, qi, ki, wi, k_sel):
    B, S, H, d = q.shape
    nb = S // Q_BLOCK
    kpos = jnp.arange(S)

    def blocks(t):
        return t.reshape((B, nb, Q_BLOCK) + t.shape[2:]).swapaxes(0, 1)

    def one_block(args):
        qb, qib, wb, t0 = args
        tq = t0 + jnp.arange(Q_BLOCK)
        limit = (tq // CHUNK + 1) * CHUNK
        logits = jnp.einsum('bqhd,bkd->bqhk', qib, ki).astype(F32) * IDX_DIM ** -0.5
        score = jnp.einsum('bqhk,bqh->bqk', jax.nn.relu(logits), wb.astype(F32) * IDX_HEADS ** -0.5)
        score = jnp.where(kpos[None, None, :] < limit[None, :, None], score, -jnp.inf)
        _, sel = lax.top_k(score, k_sel)
        valid = sel < limit[None, :, None]
        kg = jax.vmap(lambda a, i: a[i])(k, sel)
        vg = jax.vmap(lambda a, i: a[i])(v, sel)
        s = jnp.einsum('bqhd,bqkd->bqhk', qb, kg).astype(F32) * d ** -0.5
        s = jnp.where(valid[:, :, None, :], s, -jnp.inf)
        pr = jax.nn.softmax(s, axis=-1).astype(v.dtype)
        return jnp.einsum('bqhk,bqkd->bqhd', pr, vg)

    out = lax.map(one_block, (blocks(q), blocks(qi), blocks(wi), jnp.arange(nb) * Q_BLOCK))
    return out.swapaxes(0, 1).reshape(B, S, H * d)


def token_mixers(u, w_in, b_f, g_cq, g_ckv, w_uq, w_ukv, rel_table):
    B, S, _ = u.shape
    pos = jnp.arange(S, dtype=jnp.int32)
    z = u @ w_in
    splits = np.cumsum(SPLIT_SIZES)[:-1].tolist()
    (qa, ka, va, fa, cq, ckv, kr, qc, kc, vc, qd, kd, vd, qi, ki, wi) = jnp.split(z, splits, axis=-1)

    def heads(t, n):
        return t.reshape(B, S, n, -1)

    logf = jax.nn.log_sigmoid(fa.astype(F32) + b_f.astype(F32))
    decay = jnp.cumsum(logf, axis=1).transpose(0, 2, 1)
    o_a = prefix_block_attention(heads(qa, FOX_HEADS), heads(ka, FOX_HEADS), heads(va, FOX_HEADS),
                                 HEAD_DIM ** -0.5, True, decay)

    cq = rmsnorm(cq, g_cq)
    ckv = rmsnorm(ckv, g_ckv)
    qb = (cq @ w_uq).reshape(B, S, MLA_HEADS, MLA_NOPE + MLA_ROPE)
    q_b = jnp.concatenate([qb[..., :MLA_NOPE], rope(qb[..., MLA_NOPE:], pos)], axis=-1)
    kvb = (ckv @ w_ukv).reshape(B, S, MLA_HEADS, MLA_NOPE + MLA_V)
    k_r = jnp.broadcast_to(rope(kr[:, :, None, :], pos), (B, S, MLA_HEADS, MLA_ROPE))
    k_b = jnp.concatenate([kvb[..., :MLA_NOPE], k_r], axis=-1)
    o_b = prefix_block_attention(q_b, k_b, kvb[..., MLA_NOPE:], (MLA_NOPE + MLA_ROPE) ** -0.5, False)

    o_c = chunk_band_attention(heads(qc, CH_HEADS), heads(kc, CH_HEADS), heads(vc, CH_HEADS), rel_table)

    q_d = rope(heads(qd, DSA_HEADS), pos)
    k_d = rope(kd[:, :, None, :], pos)[:, :, 0]
    q_i = rope(heads(qi, IDX_HEADS), pos)
    k_i = rope(ki[:, :, None, :], pos)[:, :, 0]
    k_sel = min(TOPK_MAX, S // 4)
    o_d = dsa_attention(q_d, k_d, vd, q_i, k_i, wi, k_sel)

    return (o_a.reshape(B, S, -1), o_b.reshape(B, S, -1), o_c, o_d)


def gated_merge(u, branches, w_branch, w_gate, b_gate, w_out):
    acc = None
    for n, o in enumerate(branches):
        term = jax.nn.sigmoid(u @ w_gate[n] + b_gate[n]) * (o @ w_branch[n])
        acc = term if acc is None else acc + term
    return acc @ w_out


def setup_inputs(seed: int = 0) -> dict:
    key = jax.random.key(seed)
    ks = iter(jax.random.split(key, 40))

    def nrm(shape, fan_in):
        return jax.random.normal(next(ks), shape, F32) * fan_in ** -0.5

    def gain(width):
        return 1.0 + 0.1 * jax.random.normal(next(ks), (DEPTH, width), F32)

    L = DEPTH
    x = jax.random.normal(next(ks), (BATCH, SEQ, D_MODEL), F32)
    p = jax.random.normal(next(ks), (DEPTH, BATCH, SEQ, PLE_DIM), F32)
    w_in = nrm((L, D_MODEL, IN_COLS), D_MODEL)
    b_f = 3.0 + 0.5 * jax.random.normal(next(ks), (L, FOX_HEADS), F32)
    g_cq = gain(MLA_Q_LORA)
    g_ckv = gain(MLA_KV_LORA)
    w_uq = nrm((L, MLA_Q_LORA, MLA_HEADS * (MLA_NOPE + MLA_ROPE)), MLA_Q_LORA)
    w_ukv = nrm((L, MLA_KV_LORA, MLA_HEADS * (MLA_NOPE + MLA_V)), MLA_KV_LORA)
    rel_bias = 0.5 * jax.random.normal(next(ks), (L, CH_HEADS, 2 * MAX_REL + 1), F32)
    w_branch = nrm((L, N_BRANCH, BRANCH_WIDTH, D_MODEL), BRANCH_WIDTH)
    w_gate = nrm((L, N_BRANCH, D_MODEL, D_MODEL), D_MODEL)
    b_gate = 0.1 * jax.random.normal(next(ks), (L, N_BRANCH, D_MODEL), F32)
    w_out = nrm((L, D_MODEL, D_MODEL), D_MODEL)
    w1_gate = nrm((L, D_MODEL, D_FF), D_MODEL)
    w1_up = nrm((L, D_MODEL, D_FF), D_MODEL)
    w1_down = nrm((L, D_FF, D_MODEL), D_FF)
    w2_gate = nrm((L, D_MODEL, D_FF), D_MODEL)
    w2_up = nrm((L, D_MODEL, D_FF), D_MODEL)
    w2_down = nrm((L, D_FF, D_MODEL), D_FF)
    g_ffn1_pre = gain(D_MODEL)
    g_ffn1_post = gain(D_MODEL)
    g_mix_pre = gain(D_MODEL)
    g_mix_post = gain(D_MODEL)
    g_ffn2_pre = gain(D_MODEL)
    g_ffn2_post = gain(D_MODEL)
    g_ple_pre = gain(D_MODEL)
    g_ple_post = gain(D_MODEL)
    w_ple = nrm((L, PLE_DIM, D_MODEL), PLE_DIM)
    w_ple_gate = nrm((L, D_MODEL, D_MODEL), D_MODEL)
    return {
        "x": x, "p": p, "w_in": w_in, "b_f": b_f, "g_cq": g_cq, "g_ckv": g_ckv,
        "w_uq": w_uq, "w_ukv": w_ukv, "rel_bias": rel_bias, "w_branch": w_branch,
        "w_gate": w_gate, "b_gate": b_gate, "w_out": w_out,
        "w1_gate": w1_gate, "w1_up": w1_up, "w1_down": w1_down,
        "w2_gate": w2_gate, "w2_up": w2_up, "w2_down": w2_down,
        "g_ffn1_pre": g_ffn1_pre, "g_ffn1_post": g_ffn1_post,
        "g_mix_pre": g_mix_pre, "g_mix_post": g_mix_post,
        "g_ffn2_pre": g_ffn2_pre, "g_ffn2_post": g_ffn2_post,
        "g_ple_pre": g_ple_pre, "g_ple_post": g_ple_post,
        "w_ple": w_ple, "w_ple_gate": w_ple_gate,
    }


def reference(x, p, w_in, b_f, g_cq, g_ckv, w_uq, w_ukv, rel_bias, w_branch, w_gate, b_gate, w_out,
              w1_gate, w1_up, w1_down, w2_gate, w2_up, w2_down,
              g_ffn1_pre, g_ffn1_post, g_mix_pre, g_mix_post, g_ffn2_pre, g_ffn2_post,
              g_ple_pre, g_ple_post, w_ple, w_ple_gate):
    h = x
    for i in range(DEPTH):
        h = h + 0.5 * rmsnorm(swiglu(rmsnorm(h, g_ffn1_pre[i]), w1_gate[i], w1_up[i], w1_down[i]), g_ffn1_post[i])
        u = rmsnorm(h, g_mix_pre[i])
        branches = token_mixers(u, w_in[i], b_f[i], g_cq[i], g_ckv[i], w_uq[i], w_ukv[i], rel_bias[i])
        h = h + rmsnorm(gated_merge(u, branches, w_branch[i], w_gate[i], b_gate[i], w_out[i]), g_mix_post[i])
        h = h + 0.5 * rmsnorm(swiglu(rmsnorm(h, g_ffn2_pre[i]), w2_gate[i], w2_up[i], w2_down[i]), g_ffn2_post[i])
        gate = jax.nn.sigmoid(rmsnorm(h, g_ple_pre[i]) @ w_ple_gate[i])
        h = h + rmsnorm(gate * (p[i] @ w_ple[i]), g_ple_post[i])
    return h
```

```python
import functools

import numpy as np
import jax
import jax.numpy as jnp
from jax import lax
from jax.experimental import pallas as pl
from jax.experimental.pallas import tpu as pltpu

F32 = jnp.float32
I32 = jnp.int32
CDT = jnp.bfloat16
EPS = 1e-6
NEG = -1e30
ROPE_THETA = 10000.0

LANE = 128
HEAD_DIM = 128
N_HEADS = 8
CHUNK = 64
MLA_Q_LORA, MLA_KV_LORA, MLA_NOPE, MLA_ROPE = 768, 512, 128, 64
CH_PREV, MAX_REL = 8, 128
IDX_HEADS, IDX_DIM = 16, 64
TOPK_MAX = 256
SPLIT_SIZES = (1024, 1024, 1024, 8, 768, 512, 64, 1024, 1024, 1024, 1024, 128, 128, 1024, 64, 16)
SPLIT_NAMES = ("qa", "ka", "va", "fa", "cq", "ckv", "kr", "qc", "kc", "vc", "qd", "kd", "vd", "qi", "ki", "wi")
INT_MIN = -2147483648

VMEM_LIMIT_BYTES = 60 * 1024 * 1024


def _cparams(*sem):
    return pltpu.CompilerParams(dimension_semantics=sem, vmem_limit_bytes=VMEM_LIMIT_BYTES)


def _tile(n, pref):
    t = min(n, pref)
    assert n % t == 0, (n, pref)
    return t


def _dot(a, b):
    return jnp.dot(a, b, preferred_element_type=F32)


def _dot_nt(a, b):
    return lax.dot_general(a, b, (((1,), (1,)), ((), ())), preferred_element_type=F32)


def _rms(y, g):
    ms = jnp.mean(y * y, axis=-1, keepdims=True)
    return y * lax.rsqrt(ms + EPS) * g


def _sigmoid(x):
    return 1.0 / (1.0 + jnp.exp(-x))


def _rmsnorm_kernel(x_ref, g_ref, o_ref):
    o_ref[...] = _rms(x_ref[...], g_ref[...]).astype(o_ref.dtype)


def rmsnorm_call(x, g):
    t, d = x.shape
    tm = _tile(t, 256)
    return pl.pallas_call(
        _rmsnorm_kernel,
        grid=(t // tm,),
        in_specs=[pl.BlockSpec((tm, d), lambda i: (i, 0)), pl.BlockSpec((1, d), lambda i: (0, 0))],
        out_specs=pl.BlockSpec((tm, d), lambda i: (i, 0)),
        out_shape=jax.ShapeDtypeStruct((t, d), CDT),
        compiler_params=_cparams("parallel"),
        name="rmsnorm",
    )(x, g.reshape(1, d))


def _swiglu_kernel(a_ref, wg_ref, wu_ref, o_ref):
    a = a_ref[...]
    g = _dot(a, wg_ref[...])
    v = _dot(a, wu_ref[...])
    o_ref[...] = (g * _sigmoid(g) * v).astype(o_ref.dtype)


def swiglu_call(u, wg, wu):
    t, d = u.shape
    f = wg.shape[1]
    tm, tn = _tile(t, 1024), _tile(f, 512)
    return pl.pallas_call(
        _swiglu_kernel,
        grid=(t // tm, f // tn),
        in_specs=[pl.BlockSpec((tm, d), lambda i, j: (i, 0)),
                  pl.BlockSpec((d, tn), lambda i, j: (0, j)),
                  pl.BlockSpec((d, tn), lambda i, j: (0, j))],
        out_specs=pl.BlockSpec((tm, tn), lambda i, j: (i, j)),
        out_shape=jax.ShapeDtypeStruct((t, f), CDT),
        compiler_params=_cparams("parallel", "parallel"),
        name="swiglu_up",
    )(u, wg, wu)


ROW_CHUNK = 64


def _rowres_kernel(*refs, coef, nk, ple, emit_u):
    a_ref, w_ref, h_ref, gp_ref, gn_ref = refs[:5]
    pos = 5
    if ple:
        p_ref, wple_ref = refs[pos:pos + 2]
        pos += 2
    hout_ref = refs[pos]
    uout_ref = refs[pos + 1] if emit_u else None
    k = pl.program_id(1)
    part = _dot(a_ref[...], w_ref[...])

    @pl.when(k == 0)
    def _():
        hout_ref[...] = part

    @pl.when(k > 0)
    def _():
        hout_ref[...] += part

    @pl.when(k == nk - 1)
    def _():
        tm = hout_ref.shape[0]
        rc = min(ROW_CHUNK, tm)

        def body(r, carry):
            rows = pl.ds(pl.multiple_of(r * rc, rc), rc)
            y = hout_ref[rows, :]
            if ple:
                e = _dot(p_ref[rows, :].astype(CDT), wple_ref[...])
                y = _sigmoid(y) * e
            hn = h_ref[rows, :] + coef * _rms(y, gp_ref[...])
            hout_ref[rows, :] = hn
            if emit_u:
                uout_ref[rows, :] = _rms(hn, gn_ref[...]).astype(uout_ref.dtype)
            return carry

        lax.fori_loop(0, tm // rc, body, 0)


def rowres_call(a, w, h, g_post, g_next, coef, ple=None, emit_u=True):
    t, kdim = a.shape
    d = w.shape[1]
    tm, tk = _tile(t, 512), _tile(kdim, 512)
    nk = kdim // tk
    once = pl.Buffered(1)
    in_specs = [pl.BlockSpec((tm, tk), lambda i, k: (i, k)),
                pl.BlockSpec((tk, d), lambda i, k: (k, 0)),
                pl.BlockSpec((tm, d), lambda i, k: (i, 0), pipeline_mode=once),
                pl.BlockSpec((1, d), lambda i, k: (0, 0), pipeline_mode=once),
                pl.BlockSpec((1, d), lambda i, k: (0, 0), pipeline_mode=once)]
    args = [a, w, h, g_post.reshape(1, d), g_next.reshape(1, d)]
    if ple is not None:
        p, wple = ple
        in_specs += [pl.BlockSpec((tm, p.shape[1]), lambda i, k: (i, 0), pipeline_mode=once),
                     pl.BlockSpec(wple.shape, lambda i, k: (0, 0), pipeline_mode=once)]
        args += [p, wple]
    out_specs = [pl.BlockSpec((tm, d), lambda i, k: (i, 0))]
    out_shape = [jax.ShapeDtypeStruct((t, d), F32)]
    if emit_u:
        out_specs.append(pl.BlockSpec((tm, d), lambda i, k: (i, 0)))
        out_shape.append(jax.ShapeDtypeStruct((t, d), CDT))
    res = pl.pallas_call(
        functools.partial(_rowres_kernel, coef=coef, nk=nk, ple=ple is not None, emit_u=emit_u),
        grid=(t // tm, nk),
        in_specs=in_specs,
        out_specs=out_specs,
        out_shape=out_shape,
        compiler_params=_cparams("parallel", "arbitrary"),
        name="rowres_ple" if ple is not None else "rowres",
    )(*args)
    return (res[0], res[1]) if emit_u else (res[0], None)


def _mm_kernel(a_ref, w_ref, o_ref):
    o_ref[...] = _dot(a_ref[...], w_ref[...]).astype(o_ref.dtype)


def mm_call(a, w):
    t, d = a.shape
    n = w.shape[1]
    tm, tn = _tile(t, 1024), _tile(n, 1024)
    return pl.pallas_call(
        _mm_kernel,
        grid=(t // tm, n // tn),
        in_specs=[pl.BlockSpec((tm, d), lambda i, j: (i, 0)), pl.BlockSpec((d, tn), lambda i, j: (0, j))],
        out_specs=pl.BlockSpec((tm, tn), lambda i, j: (i, j)),
        out_shape=jax.ShapeDtypeStruct((t, n), CDT),
        compiler_params=_cparams("parallel", "parallel"),
        name="proj_plain",
    )(a, w)


def _rope_tables(seq, half):
    inv = ROPE_THETA ** (-jnp.arange(half, dtype=F32) / half)
    ang = jnp.arange(seq, dtype=F32)[:, None] * inv[None, :]
    cos, sin = jnp.cos(ang), jnp.sin(ang)
    reps = LANE // (2 * half)
    return jnp.tile(jnp.concatenate([cos, cos], -1), (1, reps)), jnp.tile(jnp.concatenate([-sin, sin], -1), (1, reps))


def _rope_block(x, c, s, half):
    if 2 * half == LANE:
        sw = pltpu.roll(x, half, axis=1)
    else:
        lane = lax.broadcasted_iota(I32, x.shape, 1)
        first = (lane & (2 * half - 1)) < half
        sw = jnp.where(first, pltpu.roll(x, LANE - half, axis=1), pltpu.roll(x, half, axis=1))
    return x * c + sw * s


def _proj_rope_kernel(a_ref, w_ref, c_ref, s_ref, o_ref, *, half, rope_blocks):
    acc = _dot(a_ref[...], w_ref[...])
    c, s = c_ref[...], s_ref[...]
    for blk in range(acc.shape[1] // LANE):
        x = acc[:, blk * LANE:(blk + 1) * LANE]
        if blk in rope_blocks:
            x = _rope_block(x, c, s, half)
        o_ref[:, blk * LANE:(blk + 1) * LANE] = x.astype(o_ref.dtype)


def proj_rope_call(u, w, cos, sin, seq, half, rope_blocks):
    t, d = u.shape
    n = w.shape[1]
    tm = _tile(seq, 512)
    ns = seq // tm
    return pl.pallas_call(
        functools.partial(_proj_rope_kernel, half=half, rope_blocks=tuple(rope_blocks)),
        grid=(t // tm,),
        in_specs=[pl.BlockSpec((tm, d), lambda i: (i, 0)),
                  pl.BlockSpec((d, n), lambda i: (0, 0)),
                  pl.BlockSpec((tm, LANE), lambda i: (i % ns, 0)),
                  pl.BlockSpec((tm, LANE), lambda i: (i % ns, 0))],
        out_specs=pl.BlockSpec((tm, n), lambda i: (i, 0)),
        out_shape=jax.ShapeDtypeStruct((t, n), CDT),
        compiler_params=_cparams("parallel"),
        name=f"proj_rope{2 * half}",
    )(u, w, cos, sin)


def _proj_mla_kernel(a_ref, w_ref, gq_ref, gkv_ref, wuq_ref, wukv_ref, c_ref, s_ref, q_ref, kv_ref):
    acc = _dot(a_ref[...], w_ref[...])
    cq = _rms(acc[:, :MLA_Q_LORA], gq_ref[...]).astype(CDT)
    ckv = _rms(acc[:, MLA_Q_LORA:MLA_Q_LORA + MLA_KV_LORA], gkv_ref[...]).astype(CDT)
    qb = _dot(cq, wuq_ref[...])
    c, s = c_ref[...], s_ref[...]
    for blk in range(qb.shape[1] // LANE):
        x = qb[:, blk * LANE:(blk + 1) * LANE]
        if blk % 2 == 1:
            x = _rope_block(x, c, s, MLA_ROPE // 2)
        q_ref[:, blk * LANE:(blk + 1) * LANE] = x.astype(q_ref.dtype)
    kv_ref[...] = _dot(ckv, wukv_ref[...]).astype(kv_ref.dtype)


def proj_mla_call(u, w, gq, gkv, wuq, wukv, cos, sin, seq):
    t, d = u.shape
    n = w.shape[1]
    tm = _tile(seq, 512)
    ns = seq // tm
    nq, nkv = wuq.shape[1], wukv.shape[1]
    return pl.pallas_call(
        _proj_mla_kernel,
        grid=(t // tm,),
        in_specs=[pl.BlockSpec((tm, d), lambda i: (i, 0)),
                  pl.BlockSpec((d, n), lambda i: (0, 0)),
                  pl.BlockSpec((1, MLA_Q_LORA), lambda i: (0, 0)),
                  pl.BlockSpec((1, MLA_KV_LORA), lambda i: (0, 0)),
                  pl.BlockSpec(wuq.shape, lambda i: (0, 0)),
                  pl.BlockSpec(wukv.shape, lambda i: (0, 0)),
                  pl.BlockSpec((tm, LANE), lambda i: (i % ns, 0)),
                  pl.BlockSpec((tm, LANE), lambda i: (i % ns, 0))],
        out_specs=[pl.BlockSpec((tm, nq), lambda i: (i, 0)), pl.BlockSpec((tm, nkv), lambda i: (i, 0))],
        out_shape=[jax.ShapeDtypeStruct((t, nq), CDT), jax.ShapeDtypeStruct((t, nkv), CDT)],
        compiler_params=_cparams("parallel"),
        name="proj_mla",
    )(u, w, gq.reshape(1, -1), gkv.reshape(1, -1), wuq, wukv, cos, sin)


TAIL_ROWS = 32


def _proj_tail_kernel(a_ref, w_ref, wt_ref, o_ref, ot_ref):
    a = a_ref[...]
    o_ref[...] = _dot(a, w_ref[...])
    ot_ref[0] = _dot_nt(wt_ref[...], a)


def proj_tail_call(u, w, wt, batch, seq):
    t, d = u.shape
    tm = _tile(seq, 512)
    ns = seq // tm
    return pl.pallas_call(
        _proj_tail_kernel,
        grid=(t // tm,),
        in_specs=[pl.BlockSpec((tm, d), lambda i: (i, 0)),
                  pl.BlockSpec((d, LANE), lambda i: (0, 0)),
                  pl.BlockSpec((TAIL_ROWS, d), lambda i: (0, 0))],
        out_specs=[pl.BlockSpec((tm, LANE), lambda i: (i, 0)),
                   pl.BlockSpec((1, TAIL_ROWS, tm), lambda i: (i // ns, 0, i % ns))],
        out_shape=[jax.ShapeDtypeStruct((t, LANE), F32), jax.ShapeDtypeStruct((batch, TAIL_ROWS, seq), F32)],
        compiler_params=_cparams("parallel"),
        name="proj_tail",
    )(u, w, wt)


def _log_sigmoid(x):
    return jnp.minimum(x, 0.0) - jnp.log1p(jnp.exp(-jnp.abs(x)))


def _decay_kernel(tail_ref, tailt_ref, bcol_ref, brow_ref, dcol_ref, drow_ref):
    seq = tail_ref.shape[0]
    x = _log_sigmoid(tail_ref[...] + bcol_ref[...])
    idx = lax.broadcasted_iota(I32, x.shape, 0)
    d = 1
    while d < seq:
        x = x + jnp.where(idx >= d, pltpu.roll(x, d, axis=0), 0.0)
        d *= 2
    dcol_ref[...] = x
    y = _log_sigmoid(tailt_ref[0, :N_HEADS, :] + brow_ref[...])
    idy = lax.broadcasted_iota(I32, y.shape, 1)
    d = 1
    while d < seq:
        y = y + jnp.where(idy >= d, pltpu.roll(y, d, axis=1), 0.0)
        d *= 2
    drow_ref[0] = y


def decay_call(tail, tailt, b_f, batch, seq):
    bcol = jnp.zeros((1, LANE), F32).at[0, :N_HEADS].set(b_f)
    brow = b_f.reshape(N_HEADS, 1)
    return pl.pallas_call(
        _decay_kernel,
        grid=(batch,),
        in_specs=[pl.BlockSpec((seq, LANE), lambda b: (b, 0)),
                  pl.BlockSpec((1, TAIL_ROWS, seq), lambda b: (b, 0, 0)),
                  pl.BlockSpec((1, LANE), lambda b: (0, 0)),
                  pl.BlockSpec((N_HEADS, 1), lambda b: (0, 0))],
        out_specs=[pl.BlockSpec((seq, LANE), lambda b: (b, 0)),
                   pl.BlockSpec((1, N_HEADS, seq), lambda b: (b, 0, 0))],
        out_shape=[jax.ShapeDtypeStruct((batch * seq, LANE), F32), jax.ShapeDtypeStruct((batch, N_HEADS, seq), F32)],
        compiler_params=_cparams("parallel"),
        name="decay_scan",
    )(tail, tailt, bcol, brow)


def _softmax_step(carry, s, v):
    m, l, acc = carry
    m_new = jnp.maximum(m, jnp.max(s, axis=-1, keepdims=True))
    alpha = jnp.exp(m - m_new)
    p = jnp.exp(s - m_new)
    l = alpha * l + jnp.sum(p, axis=-1, keepdims=True)
    acc = alpha * acc + _dot(p.astype(CDT), v)
    return m_new, l, acc


def _softmax_init(tq, dv):
    return jnp.full((tq, 1), NEG, F32), jnp.zeros((tq, 1), F32), jnp.zeros((tq, dv), F32)


def _fox_kernel(q_ref, k_ref, v_ref, dcol_ref, drow_ref, o_ref, *, tq, scale):
    i = pl.program_id(1)
    row = lax.broadcasted_iota(I32, (tq, tq), 0) + i * tq
    col = lax.broadcasted_iota(I32, (tq, tq), 1)
    for h in range(N_HEADS):
        hs = slice(h * HEAD_DIM, (h + 1) * HEAD_DIM)
        q = q_ref[:, hs]
        dq = dcol_ref[:, h:h + 1]

        def body(c, carry, hs=hs, q=q, dq=dq, h=h):
            ks = pl.ds(pl.multiple_of(c * tq, tq), tq)
            s = _dot_nt(q, k_ref[ks, hs]) * scale
            s = s + dq - drow_ref[h, pl.ds(c, 1), :]
            s = jnp.where(col + c * tq <= row, s, NEG)
            return _softmax_step(carry, s, v_ref[ks, hs])

        m, l, acc = lax.fori_loop(0, i + 1, body, _softmax_init(tq, HEAD_DIM))
        o_ref[:, hs] = (acc / l).astype(o_ref.dtype)


def fox_call(z1, dcol, drow, batch, seq):
    tq = _tile(seq, 256)
    nq = seq // tq
    drow = drow.reshape(batch * N_HEADS, nq, tq)
    w = N_HEADS * HEAD_DIM
    return pl.pallas_call(
        functools.partial(_fox_kernel, tq=tq, scale=HEAD_DIM ** -0.5),
        grid=(batch, nq),
        in_specs=[pl.BlockSpec((tq, w), lambda b, i: (b * nq + i, 0)),
                  pl.BlockSpec((seq, w), lambda b, i: (b, 1)),
                  pl.BlockSpec((seq, w), lambda b, i: (b, 2)),
                  pl.BlockSpec((tq, LANE), lambda b, i: (b * nq + i, 0)),
                  pl.BlockSpec((N_HEADS, nq, tq), lambda b, i: (b, 0, 0))],
        out_specs=pl.BlockSpec((tq, w), lambda b, i: (b * nq + i, 0)),
        out_shape=jax.ShapeDtypeStruct((batch * seq, w), CDT),
        compiler_params=_cparams("parallel", "parallel"),
        name="attn_fox",
    )(z1, z1, z1, dcol, drow)


def _mla_kernel(q_ref, kv_ref, kr_ref, o_ref, *, tq, scale):
    i = pl.program_id(1)
    rowc = (lax.broadcasted_iota(I32, (tq, tq), 0) + i * tq) >> 6
    col = lax.broadcasted_iota(I32, (tq, tq), 1)
    for h in range(N_HEADS):
        b0 = 2 * h * LANE
        qn = q_ref[:, b0:b0 + LANE]
        qr = q_ref[:, b0 + LANE:b0 + 2 * LANE]

        def body(c, carry, b0=b0, qn=qn, qr=qr):
            ks = pl.ds(pl.multiple_of(c * tq, tq), tq)
            s = (_dot_nt(qn, kv_ref[ks, b0:b0 + LANE]) + _dot_nt(qr, kr_ref[ks, :])) * scale
            s = jnp.where(((col + c * tq) >> 6) <= rowc, s, NEG)
            return _softmax_step(carry, s, kv_ref[ks, b0 + LANE:b0 + 2 * LANE])

        m, l, acc = lax.fori_loop(0, i + 1, body, _softmax_init(tq, HEAD_DIM))
        o_ref[:, h * HEAD_DIM:(h + 1) * HEAD_DIM] = (acc / l).astype(o_ref.dtype)


def mla_call(qm, kvb, z3, kr_block, batch, seq):
    tq = _tile(seq, 256)
    nq = seq // tq
    w = N_HEADS * HEAD_DIM
    return pl.pallas_call(
        functools.partial(_mla_kernel, tq=tq, scale=(MLA_NOPE + MLA_ROPE) ** -0.5),
        grid=(batch, nq),
        in_specs=[pl.BlockSpec((tq, qm.shape[1]), lambda b, i: (b * nq + i, 0)),
                  pl.BlockSpec((seq, kvb.shape[1]), lambda b, i: (b, 0)),
                  pl.BlockSpec((seq, LANE), lambda b, i: (b, kr_block))],
        out_specs=pl.BlockSpec((tq, w), lambda b, i: (b * nq + i, 0)),
        out_shape=jax.ShapeDtypeStruct((batch * seq, w), CDT),
        compiler_params=_cparams("parallel", "parallel"),
        name="attn_mla",
    )(qm, kvb, z3)


BAND_TQ = 256
BAND_NKB = (CH_PREV * CHUNK) // BAND_TQ + 1


def _band_bias_tiles(rel_table):
    q = np.arange(BAND_TQ)[:, None]
    koff = np.arange(BAND_NKB * BAND_TQ)[None, :] - (BAND_NKB - 1) * BAND_TQ
    kc = np.floor_divide(koff, CHUNK)
    qc = q // CHUNK
    band = (kc >= qc - CH_PREV) & (kc <= qc)
    rel = np.clip(q - koff, -MAX_REL, MAX_REL) + MAX_REL
    bias = jnp.where(jnp.asarray(band)[None], rel_table.astype(F32)[:, jnp.asarray(rel)], NEG)
    return bias.reshape(N_HEADS, BAND_TQ, BAND_NKB, BAND_TQ).transpose(0, 2, 1, 3)


def _band_kernel(q_ref, k_ref, v_ref, bias_ref, o_ref, *, tq, scale):
    i = pl.program_id(1)
    for h in range(N_HEADS):
        hs = slice(h * HEAD_DIM, (h + 1) * HEAD_DIM)
        q = q_ref[:, hs]
        ss, vs = [], []
        for kb in range(BAND_NKB):
            kt = i - (BAND_NKB - 1) + kb
            ks = pl.ds(pl.multiple_of(jnp.maximum(kt, 0) * tq, tq), tq)
            s = _dot_nt(q, k_ref[ks, hs]) * scale + bias_ref[h, kb]
            if kb < BAND_NKB - 1:
                s = jnp.where(kt >= 0, s, NEG)
            ss.append(s)
            vs.append(v_ref[ks, hs])
        m = ss[0].max(axis=-1, keepdims=True)
        for s in ss[1:]:
            m = jnp.maximum(m, s.max(axis=-1, keepdims=True))
        l = jnp.zeros((tq, 1), F32)
        acc = jnp.zeros((tq, HEAD_DIM), F32)
        for s, v in zip(ss, vs):
            p = jnp.exp(s - m)
            l = l + p.sum(axis=-1, keepdims=True)
            acc = acc + _dot(p.astype(CDT), v)
        o_ref[:, hs] = (acc / l).astype(o_ref.dtype)


def band_call(z1, bias_tiles, batch, seq):
    tq = BAND_TQ
    assert seq % tq == 0
    nq = seq // tq
    w = N_HEADS * HEAD_DIM
    return pl.pallas_call(
        functools.partial(_band_kernel, tq=tq, scale=HEAD_DIM ** -0.5),
        grid=(batch, nq),
        in_specs=[pl.BlockSpec((tq, w), lambda b, i: (b * nq + i, 3)),
                  pl.BlockSpec((seq, w), lambda b, i: (b, 4)),
                  pl.BlockSpec((seq, w), lambda b, i: (b, 5)),
                  pl.BlockSpec(bias_tiles.shape, lambda b, i: (0, 0, 0, 0))],
        out_specs=pl.BlockSpec((tq, w), lambda b, i: (b * nq + i, 0)),
        out_shape=jax.ShapeDtypeStruct((batch * seq, w), CDT),
        compiler_params=_cparams("parallel", "parallel"),
        name="attn_band",
    )(z1, z1, z1, bias_tiles)


DSA_TQ = 128


def _count(mask):
    return jnp.sum(mask.astype(F32), axis=-1, keepdims=True)


def _dsa_kernel(qd_ref, kd_ref, vd_ref, qi_ref, ka_ref, kb_ref, wi_ref, o_ref, *, tq, seq, k_sel, scale):
    i = pl.program_id(1)
    w = wi_ref[...] * (IDX_HEADS ** -0.5)
    ka, kb = ka_ref[...], kb_ref[...]
    score = jnp.zeros((tq, seq), F32)
    for p in range(IDX_HEADS // 2):
        qp = qi_ref[:, p * LANE:(p + 1) * LANE]
        l0 = _dot_nt(qp, ka) * (IDX_DIM ** -0.5)
        l1 = _dot_nt(qp, kb) * (IDX_DIM ** -0.5)
        c0 = N_HEADS + 2 * p
        score = score + jnp.maximum(l0, 0.0) * w[:, c0:c0 + 1] + jnp.maximum(l1, 0.0) * w[:, c0 + 1:c0 + 2]
    score = score + 0.0
    col = lax.broadcasted_iota(I32, (tq, seq), 1)
    tpos = lax.broadcasted_iota(I32, (tq, 1), 0) + i * tq
    limit = ((tpos >> 6) + 1) << 6
    valid = col < limit
    bits = lax.bitcast_convert_type(score, I32)
    key = jnp.where(bits < 0, bits ^ 0x7FFFFFFF, bits)
    key = jnp.where(valid, key, INT_MIN)

    def thr_body(it, tu):
        cand = tu | lax.shift_left(jnp.int32(1), 31 - it)
        cnt = _count(key >= (cand ^ INT_MIN))
        return jnp.where(cnt >= k_sel, cand, tu)

    tu = lax.fori_loop(0, 32, thr_body, jnp.zeros((tq, 1), I32))
    thr = tu ^ INT_MIN
    gt = key > thr
    ties = key == thr
    need = k_sel - _count(gt)

    nbits = max(1, int(np.ceil(np.log2(seq))))

    def tie_body(it, j):
        cand = j | lax.shift_left(jnp.int32(1), nbits - 1 - it)
        cnt = _count(ties & (col < cand))
        return jnp.where(cnt < need, cand, j)

    j = lax.fori_loop(0, nbits, tie_body, jnp.zeros((tq, 1), I32))
    sel = valid & (gt | (ties & (col <= j)))

    kd, vd = kd_ref[...], vd_ref[...]
    for h in range(N_HEADS):
        hs = slice(h * HEAD_DIM, (h + 1) * HEAD_DIM)
        s = jnp.where(sel, _dot_nt(qd_ref[:, hs], kd) * scale, NEG)
        m = jnp.max(s, axis=-1, keepdims=True)
        p = jnp.exp(s - m)
        l = jnp.sum(p, axis=-1, keepdims=True)
        o_ref[:, hs] = (_dot(p.astype(CDT), vd) / l).astype(o_ref.dtype)


def dsa_call(z2, z3, tail, batch, seq):
    tq = _tile(seq, DSA_TQ)
    nq = seq // tq
    w = N_HEADS * HEAD_DIM
    nb = w // LANE
    k_sel = min(TOPK_MAX, seq // 4)
    return pl.pallas_call(
        functools.partial(_dsa_kernel, tq=tq, seq=seq, k_sel=k_sel, scale=HEAD_DIM ** -0.5),
        grid=(batch, nq),
        in_specs=[pl.BlockSpec((tq, w), lambda b, i: (b * nq + i, 0)),
                  pl.BlockSpec((seq, LANE), lambda b, i: (b, nb)),
                  pl.BlockSpec((seq, LANE), lambda b, i: (b, nb + 1)),
                  pl.BlockSpec((tq, w), lambda b, i: (b * nq + i, 0)),
                  pl.BlockSpec((seq, LANE), lambda b, i: (b, nb)),
                  pl.BlockSpec((seq, LANE), lambda b, i: (b, nb + 1)),
                  pl.BlockSpec((tq, LANE), lambda b, i: (b * nq + i, 0))],
        out_specs=pl.BlockSpec((tq, w), lambda b, i: (b * nq + i, 0)),
        out_shape=jax.ShapeDtypeStruct((batch * seq, w), CDT),
        compiler_params=_cparams("parallel", "parallel"),
        name="attn_dsa",
    )(z2, z2, z2, z3, z3, z3, tail)


def _merge_kernel(u_ref, o_ref, wg_ref, bg_ref, wb_ref, out_ref, acc_ref, *, nbranch):
    n = pl.program_id(2)
    g = _dot(u_ref[...], wg_ref[0]) + bg_ref[0]
    term = _sigmoid(g) * _dot(o_ref[...], wb_ref[0])

    @pl.when(n == 0)
    def _():
        acc_ref[...] = term

    @pl.when(n > 0)
    def _():
        acc_ref[...] += term

    @pl.when(n == nbranch - 1)
    def _():
        out_ref[...] = acc_ref[...].astype(out_ref.dtype)


def merge_call(u, o_all, wg, bg, wb):
    t, d = u.shape
    nbranch, bw = wb.shape[0], wb.shape[1]
    tm, tn = _tile(t, 1024), _tile(d, 512)
    return pl.pallas_call(
        functools.partial(_merge_kernel, nbranch=nbranch),
        grid=(t // tm, d // tn, nbranch),
        in_specs=[pl.BlockSpec((tm, d), lambda i, j, n: (i, 0)),
                  pl.BlockSpec((tm, bw), lambda i, j, n: (i, n)),
                  pl.BlockSpec((1, d, tn), lambda i, j, n: (n, 0, j)),
                  pl.BlockSpec((1, 1, tn), lambda i, j, n: (n, 0, j)),
                  pl.BlockSpec((1, bw, tn), lambda i, j, n: (n, 0, j))],
        out_specs=pl.BlockSpec((tm, tn), lambda i, j, n: (i, j)),
        out_shape=jax.ShapeDtypeStruct((t, d), CDT),
        scratch_shapes=[pltpu.VMEM((tm, tn), F32)],
        compiler_params=_cparams("parallel", "parallel", "arbitrary"),
        name="gated_merge",
    )(u, o_all, wg, bg, wb)


def _pack_w_in(w_in):
    off = np.concatenate([[0], np.cumsum(SPLIT_SIZES)])
    col = {n: w_in[:, off[k]:off[k + 1]] for k, n in enumerate(SPLIT_NAMES)}
    d = w_in.shape[0]
    z64 = jnp.zeros((d, 64), w_in.dtype)
    w1 = jnp.concatenate([col[n] for n in ("qa", "ka", "va", "qc", "kc", "vc")], axis=1)
    w2 = jnp.concatenate([col["qd"], col["kd"], col["vd"]], axis=1)
    w3 = jnp.concatenate([col["qi"], col["ki"], z64, z64, col["ki"], col["kr"], z64], axis=1)
    w4 = jnp.concatenate([col["cq"], col["ckv"]], axis=1)
    w5 = jnp.concatenate([col["fa"], col["wi"], jnp.zeros((d, LANE - 24), w_in.dtype)], axis=1)
    w5t = jnp.concatenate([w5[:, :24], jnp.zeros((d, TAIL_ROWS - 24), w_in.dtype)], axis=1).T
    return [w.astype(CDT) for w in (w1, w2, w3, w4, w5, w5t)]


def _pack_w_uq(w_uq):
    r = w_uq.shape[0]
    w = w_uq.reshape(r, N_HEADS, MLA_NOPE + MLA_ROPE)
    w = jnp.concatenate([w, jnp.zeros((r, N_HEADS, 2 * LANE - MLA_NOPE - MLA_ROPE), w_uq.dtype)], axis=-1)
    return w.reshape(r, N_HEADS * 2 * LANE).astype(CDT)


def kernel(x, p, w_in, b_f, g_cq, g_ckv, w_uq, w_ukv, rel_bias, w_branch, w_gate, b_gate, w_out, w1_gate, w1_up, w1_down, w2_gate, w2_up, w2_down, g_ffn1_pre, g_ffn1_post, g_mix_pre, g_mix_post, g_ffn2_pre, g_ffn2_post, g_ple_pre, g_ple_post, w_ple, w_ple_gate):
    batch, seq, d = x.shape
    depth = w_in.shape[0]
    t = batch * seq
    cos128, sin128 = _rope_tables(seq, HEAD_DIM // 2)
    cos64, sin64 = _rope_tables(seq, IDX_DIM // 2)
    nb = N_HEADS * HEAD_DIM // LANE

    h = x.reshape(t, d)
    u = rmsnorm_call(h, g_ffn1_pre[0])
    for i in range(depth):
        a = swiglu_call(u, w1_gate[i].astype(CDT), w1_up[i].astype(CDT))
        h, u = rowres_call(a, w1_down[i].astype(CDT), h, g_ffn1_post[i], g_mix_pre[i], 0.5)

        w1, w2, w3, w4, w5, w5t = _pack_w_in(w_in[i])
        z1 = mm_call(u, w1)
        z2 = proj_rope_call(u, w2, cos128, sin128, seq, HEAD_DIM // 2, range(nb + 1))
        z3 = proj_rope_call(u, w3, cos64, sin64, seq, IDX_DIM // 2, range(nb + 3))
        qm, kvb = proj_mla_call(u, w4, g_cq[i], g_ckv[i], _pack_w_uq(w_uq[i]), w_ukv[i].astype(CDT), cos64, sin64, seq)
        tail, tailt = proj_tail_call(u, w5, w5t, batch, seq)
        dcol, drow = decay_call(tail, tailt, b_f[i], batch, seq)
        o_a = fox_call(z1, dcol, drow, batch, seq)
        o_b = mla_call(qm, kvb, z3, nb + 2, batch, seq)
        o_c = band_call(z1, _band_bias_tiles(rel_bias[i]), batch, seq)
        o_d = dsa_call(z2, z3, tail, batch, seq)
        o_all = jnp.concatenate([o_a, o_b, o_c, o_d], axis=1)
        merged = merge_call(u, o_all, w_gate[i].astype(CDT), b_gate[i].reshape(-1, 1, d), w_branch[i].astype(CDT))
        h, u = rowres_call(merged, w_out[i].astype(CDT), h, g_mix_post[i], g_ffn2_pre[i], 1.0)

        a = swiglu_call(u, w2_gate[i].astype(CDT), w2_up[i].astype(CDT))
        h, u = rowres_call(a, w2_down[i].astype(CDT), h, g_ffn2_post[i], g_ple_pre[i], 0.5)

        last = i == depth - 1
        g_next = g_ple_pre[i] if last else g_ffn1_pre[i + 1]
        h, u = rowres_call(u, w_ple_gate[i].astype(CDT), h, g_ple_post[i], g_next, 1.0,
                           ple=(p[i].reshape(t, -1), w_ple[i].astype(CDT)), emit_u=not last)
    return h.reshape(batch, seq, d)
```

```python
import functools

import numpy as np
import jax
import jax.numpy as jnp
from jax import lax
from jax.experimental import pallas as pl
from jax.experimental.pallas import tpu as pltpu

F32 = jnp.float32
I32 = jnp.int32
CDT = jnp.bfloat16
EPS = 1e-6
NEG = -1e30
ROPE_THETA = 10000.0

LANE = 128
HEAD_DIM = 128
N_HEADS = 8
HB = N_HEADS * HEAD_DIM // LANE
CHUNK = 64
MLA_Q_LORA, MLA_KV_LORA, MLA_NOPE, MLA_ROPE = 768, 512, 128, 64
CH_PREV, MAX_REL = 8, 128
IDX_HEADS, IDX_DIM = 16, 64
TOPK_MAX = 256
SPLIT_SIZES = (1024, 1024, 1024, 8, 768, 512, 64, 1024, 1024, 1024, 1024, 128, 128, 1024, 64, 16)
SPLIT_NAMES = ("qa", "ka", "va", "fa", "cq", "ckv", "kr", "qc", "kc", "vc", "qd", "kd", "vd", "qi", "ki", "wi")
INT_MIN = -2147483648

VMEM_LIMIT_BYTES = 60 * 1024 * 1024


def _cparams(*sem):
    return pltpu.CompilerParams(dimension_semantics=sem, vmem_limit_bytes=VMEM_LIMIT_BYTES)


def _tile(n, pref):
    t = min(n, pref)
    assert n % t == 0, (n, pref)
    return t


def _dot(a, b):
    return jnp.dot(a, b, preferred_element_type=F32)


def _dot_nt(a, b):
    return lax.dot_general(a, b, (((1,), (1,)), ((), ())), preferred_element_type=F32)


def _rms(y, g):
    ms = jnp.mean(y * y, axis=-1, keepdims=True)
    return y * lax.rsqrt(ms + EPS) * g


def _sigmoid(x):
    return 1.0 / (1.0 + jnp.exp(-x))


def _rmsnorm_kernel(x_ref, g_ref, o_ref):
    o_ref[...] = _rms(x_ref[...], g_ref[...]).astype(o_ref.dtype)


def rmsnorm_call(x, g):
    t, d = x.shape
    tm = _tile(t, 256)
    return pl.pallas_call(
        _rmsnorm_kernel,
        grid=(t // tm,),
        in_specs=[pl.BlockSpec((tm, d), lambda i: (i, 0)), pl.BlockSpec((1, d), lambda i: (0, 0))],
        out_specs=pl.BlockSpec((tm, d), lambda i: (i, 0)),
        out_shape=jax.ShapeDtypeStruct((t, d), CDT),
        compiler_params=_cparams("parallel"),
        name="rmsnorm",
    )(x, g.reshape(1, d))


def _swiglu_kernel(a_ref, wg_ref, wu_ref, o_ref):
    a = a_ref[...]
    g = _dot(a, wg_ref[...])
    v = _dot(a, wu_ref[...])
    o_ref[...] = (g * _sigmoid(g) * v).astype(o_ref.dtype)


def swiglu_call(u, wg, wu):
    t, d = u.shape
    f = wg.shape[1]
    tm, tn = _tile(t, 1024), _tile(f, 512)
    return pl.pallas_call(
        _swiglu_kernel,
        grid=(t // tm, f // tn),
        in_specs=[pl.BlockSpec((tm, d), lambda i, j: (i, 0)),
                  pl.BlockSpec((d, tn), lambda i, j: (0, j)),
                  pl.BlockSpec((d, tn), lambda i, j: (0, j))],
        out_specs=pl.BlockSpec((tm, tn), lambda i, j: (i, j)),
        out_shape=jax.ShapeDtypeStruct((t, f), CDT),
        compiler_params=_cparams("parallel", "parallel"),
        name="swiglu_up",
    )(u, wg, wu)


ROW_CHUNK = 64
ROWRES_TK = 4096


def _rowres_kernel(*refs, coef, nk, nj, tn, ple, emit_u):
    a_ref, w_ref, h_ref, gp_ref, gn_ref = refs[:5]
    pos = 5
    if ple:
        p_ref, wple_ref = refs[pos:pos + 2]
        pos += 2
    hout_ref = refs[pos]
    uout_ref = refs[pos + 1] if emit_u else None
    k, j = pl.program_id(1), pl.program_id(2)
    cols = pl.ds(pl.multiple_of(j * tn, tn), tn)
    part = _dot(a_ref[...], w_ref[...])
    if nk == 1:
        hout_ref[:, cols] = part
    else:
        @pl.when(k == 0)
        def _():
            hout_ref[:, cols] = part

        @pl.when(k > 0)
        def _():
            hout_ref[:, cols] += part

    @pl.when((k == nk - 1) & (j == nj - 1))
    def _():
        tm = hout_ref.shape[0]
        rc = min(ROW_CHUNK, tm)

        def body(r, carry):
            rows = pl.ds(pl.multiple_of(r * rc, rc), rc)
            y = hout_ref[rows, :]
            if ple:
                e = _dot(p_ref[rows, :].astype(CDT), wple_ref[...])
                y = _sigmoid(y) * e
            hn = h_ref[rows, :] + coef * _rms(y, gp_ref[...])
            hout_ref[rows, :] = hn
            if emit_u:
                uout_ref[rows, :] = _rms(hn, gn_ref[...]).astype(uout_ref.dtype)
            return carry

        lax.fori_loop(0, tm // rc, body, 0)


def rowres_call(a, w, h, g_post, g_next, coef, ple=None, emit_u=True):
    t, kdim = a.shape
    d = w.shape[1]
    tm, tk, tn = _tile(t, 512), _tile(kdim, ROWRES_TK), _tile(d, 512)
    nk, nj = kdim // tk, d // tn
    once = pl.Buffered(1)
    in_specs = [pl.BlockSpec((tm, tk), lambda i, k, j: (i, k)),
                pl.BlockSpec((tk, tn), lambda i, k, j: (k, j)),
                pl.BlockSpec((tm, d), lambda i, k, j: (i, 0), pipeline_mode=once),
                pl.BlockSpec((1, d), lambda i, k, j: (0, 0), pipeline_mode=once),
                pl.BlockSpec((1, d), lambda i, k, j: (0, 0), pipeline_mode=once)]
    args = [a, w, h, g_post.reshape(1, d), g_next.reshape(1, d)]
    if ple is not None:
        p, wple = ple
        in_specs += [pl.BlockSpec((tm, p.shape[1]), lambda i, k, j: (i, 0), pipeline_mode=once),
                     pl.BlockSpec(wple.shape, lambda i, k, j: (0, 0), pipeline_mode=once)]
        args += [p, wple]
    out_specs = [pl.BlockSpec((tm, d), lambda i, k, j: (i, 0))]
    out_shape = [jax.ShapeDtypeStruct((t, d), F32)]
    if emit_u:
        out_specs.append(pl.BlockSpec((tm, d), lambda i, k, j: (i, 0)))
        out_shape.append(jax.ShapeDtypeStruct((t, d), CDT))
    res = pl.pallas_call(
        functools.partial(_rowres_kernel, coef=coef, nk=nk, nj=nj, tn=tn, ple=ple is not None, emit_u=emit_u),
        grid=(t // tm, nk, nj),
        in_specs=in_specs,
        out_specs=out_specs,
        out_shape=out_shape,
        compiler_params=_cparams("parallel", "arbitrary", "arbitrary"),
        name="rowres_ple" if ple is not None else "rowres",
    )(*args)
    return (res[0], res[1]) if emit_u else (res[0], None)


def _mm_kernel(a_ref, w_ref, o_ref):
    acc = _dot(a_ref[...], w_ref[...])
    for blk in range(o_ref.shape[0]):
        o_ref[blk] = acc[:, blk * LANE:(blk + 1) * LANE].astype(o_ref.dtype)


def mm_call(a, w):
    t, d = a.shape
    n = w.shape[1]
    tm, tn = _tile(t, 1024), _tile(n, 1024)
    return pl.pallas_call(
        _mm_kernel,
        grid=(t // tm, n // tn),
        in_specs=[pl.BlockSpec((tm, d), lambda i, j: (i, 0)), pl.BlockSpec((d, tn), lambda i, j: (0, j))],
        out_specs=pl.BlockSpec((tn // LANE, tm, LANE), lambda i, j: (j, i, 0)),
        out_shape=jax.ShapeDtypeStruct((n // LANE, t, LANE), CDT),
        compiler_params=_cparams("parallel", "parallel"),
        name="proj_plain",
    )(a, w)


def _rope_tables(seq, half):
    inv = ROPE_THETA ** (-jnp.arange(half, dtype=F32) / half)
    ang = jnp.arange(seq, dtype=F32)[:, None] * inv[None, :]
    cos, sin = jnp.cos(ang), jnp.sin(ang)
    reps = LANE // (2 * half)
    return jnp.tile(jnp.concatenate([cos, cos], -1), (1, reps)), jnp.tile(jnp.concatenate([-sin, sin], -1), (1, reps))


def _rope_block(x, c, s, half):
    if 2 * half == LANE:
        sw = pltpu.roll(x, half, axis=1)
    else:
        lane = lax.broadcasted_iota(I32, x.shape, 1)
        first = (lane & (2 * half - 1)) < half
        sw = jnp.where(first, pltpu.roll(x, LANE - half, axis=1), pltpu.roll(x, half, axis=1))
    return x * c + sw * s


def _proj_rope_kernel(a_ref, w_ref, c_ref, s_ref, o_ref, *, half, rope_blocks):
    acc = _dot(a_ref[...], w_ref[...])
    c, s = c_ref[...], s_ref[...]
    for blk in range(o_ref.shape[0]):
        x = acc[:, blk * LANE:(blk + 1) * LANE]
        if blk in rope_blocks:
            x = _rope_block(x, c, s, half)
        o_ref[blk] = x.astype(o_ref.dtype)


def proj_rope_call(u, w, cos, sin, seq, half, rope_blocks):
    t, d = u.shape
    n = w.shape[1]
    tm = _tile(seq, 512)
    ns = seq // tm
    return pl.pallas_call(
        functools.partial(_proj_rope_kernel, half=half, rope_blocks=tuple(rope_blocks)),
        grid=(t // tm,),
        in_specs=[pl.BlockSpec((tm, d), lambda i: (i, 0)),
                  pl.BlockSpec((d, n), lambda i: (0, 0)),
                  pl.BlockSpec((tm, LANE), lambda i: (i % ns, 0)),
                  pl.BlockSpec((tm, LANE), lambda i: (i % ns, 0))],
        out_specs=pl.BlockSpec((n // LANE, tm, LANE), lambda i: (0, i, 0)),
        out_shape=jax.ShapeDtypeStruct((n // LANE, t, LANE), CDT),
        compiler_params=_cparams("parallel"),
        name=f"proj_rope{2 * half}",
    )(u, w, cos, sin)


def _proj_mla_kernel(a_ref, w_ref, gq_ref, gkv_ref, wuq_ref, wukv_ref, c_ref, s_ref, q_ref, kv_ref):
    acc = _dot(a_ref[...], w_ref[...])
    cq = _rms(acc[:, :MLA_Q_LORA], gq_ref[...]).astype(CDT)
    ckv = _rms(acc[:, MLA_Q_LORA:MLA_Q_LORA + MLA_KV_LORA], gkv_ref[...]).astype(CDT)
    qb = _dot(cq, wuq_ref[...])
    c, s = c_ref[...], s_ref[...]
    for blk in range(q_ref.shape[0]):
        x = qb[:, blk * LANE:(blk + 1) * LANE]
        if blk % 2 == 1:
            x = _rope_block(x, c, s, MLA_ROPE // 2)
        q_ref[blk] = x.astype(q_ref.dtype)
    kvb = _dot(ckv, wukv_ref[...])
    for blk in range(kv_ref.shape[0]):
        kv_ref[blk] = kvb[:, blk * LANE:(blk + 1) * LANE].astype(kv_ref.dtype)


def proj_mla_call(u, w, gq, gkv, wuq, wukv, cos, sin, seq):
    t, d = u.shape
    n = w.shape[1]
    tm = _tile(seq, 512)
    ns = seq // tm
    nq, nkv = wuq.shape[1] // LANE, wukv.shape[1] // LANE
    return pl.pallas_call(
        _proj_mla_kernel,
        grid=(t // tm,),
        in_specs=[pl.BlockSpec((tm, d), lambda i: (i, 0)),
                  pl.BlockSpec((d, n), lambda i: (0, 0)),
                  pl.BlockSpec((1, MLA_Q_LORA), lambda i: (0, 0)),
                  pl.BlockSpec((1, MLA_KV_LORA), lambda i: (0, 0)),
                  pl.BlockSpec(wuq.shape, lambda i: (0, 0)),
                  pl.BlockSpec(wukv.shape, lambda i: (0, 0)),
                  pl.BlockSpec((tm, LANE), lambda i: (i % ns, 0)),
                  pl.BlockSpec((tm, LANE), lambda i: (i % ns, 0))],
        out_specs=[pl.BlockSpec((nq, tm, LANE), lambda i: (0, i, 0)), pl.BlockSpec((nkv, tm, LANE), lambda i: (0, i, 0))],
        out_shape=[jax.ShapeDtypeStruct((nq, t, LANE), CDT), jax.ShapeDtypeStruct((nkv, t, LANE), CDT)],
        compiler_params=_cparams("parallel"),
        name="proj_mla",
    )(u, w, gq.reshape(1, -1), gkv.reshape(1, -1), wuq, wukv, cos, sin)


TAIL_ROWS = 32


def _proj_tail_kernel(a_ref, w_ref, wt_ref, o_ref, ot_ref):
    a = a_ref[...]
    o_ref[...] = _dot(a, w_ref[...])
    ot_ref[0] = _dot_nt(wt_ref[...], a)


def proj_tail_call(u, w, wt, batch, seq):
    t, d = u.shape
    tm = _tile(seq, 512)
    ns = seq // tm
    return pl.pallas_call(
        _proj_tail_kernel,
        grid=(t // tm,),
        in_specs=[pl.BlockSpec((tm, d), lambda i: (i, 0)),
                  pl.BlockSpec((d, LANE), lambda i: (0, 0)),
                  pl.BlockSpec((TAIL_ROWS, d), lambda i: (0, 0))],
        out_specs=[pl.BlockSpec((tm, LANE), lambda i: (i, 0)),
                   pl.BlockSpec((1, TAIL_ROWS, tm), lambda i: (i // ns, 0, i % ns))],
        out_shape=[jax.ShapeDtypeStruct((t, LANE), F32), jax.ShapeDtypeStruct((batch, TAIL_ROWS, seq), F32)],
        compiler_params=_cparams("parallel"),
        name="proj_tail",
    )(u, w, wt)


def _log_sigmoid(x):
    return jnp.minimum(x, 0.0) - jnp.log1p(jnp.exp(-jnp.abs(x)))


def _decay_kernel(tail_ref, tailt_ref, bcol_ref, brow_ref, dq_ref, drow_ref):
    seq = tail_ref.shape[0]
    x = _log_sigmoid(tail_ref[...] + bcol_ref[...])
    idx = lax.broadcasted_iota(I32, x.shape, 0)
    d = 1
    while d < seq:
        x = x + jnp.where(idx >= d, pltpu.roll(x, d, axis=0), 0.0)
        d *= 2
    for h in range(N_HEADS):
        dq_ref[h] = jnp.broadcast_to(x[:, h:h + 1], (seq, LANE))
    y = _log_sigmoid(tailt_ref[0, :N_HEADS, :] + brow_ref[...])
    idy = lax.broadcasted_iota(I32, y.shape, 1)
    d = 1
    while d < seq:
        y = y + jnp.where(idy >= d, pltpu.roll(y, d, axis=1), 0.0)
        d *= 2
    drow_ref[0] = y


def decay_call(tail, tailt, b_f, batch, seq):
    bcol = jnp.zeros((1, LANE), F32).at[0, :N_HEADS].set(b_f)
    brow = b_f.reshape(N_HEADS, 1)
    return pl.pallas_call(
        _decay_kernel,
        grid=(batch,),
        in_specs=[pl.BlockSpec((seq, LANE), lambda b: (b, 0)),
                  pl.BlockSpec((1, TAIL_ROWS, seq), lambda b: (b, 0, 0)),
                  pl.BlockSpec((1, LANE), lambda b: (0, 0)),
                  pl.BlockSpec((N_HEADS, 1), lambda b: (0, 0))],
        out_specs=[pl.BlockSpec((N_HEADS, seq, LANE), lambda b: (0, b, 0)),
                   pl.BlockSpec((1, N_HEADS, seq), lambda b: (b, 0, 0))],
        out_shape=[jax.ShapeDtypeStruct((N_HEADS, batch * seq, LANE), F32),
                   jax.ShapeDtypeStruct((batch, N_HEADS, seq), F32)],
        compiler_params=_cparams("parallel"),
        name="decay_scan",
    )(tail, tailt, bcol, brow)


ATT_TQ = 512
ATT_SUB = 2


def _softmax_step(carry, s, v):
    m, l, acc = carry
    m_new = jnp.maximum(m, jnp.max(s, axis=-1, keepdims=True))
    alpha = jnp.exp(m - m_new)
    p = jnp.exp(s - m_new)
    l = alpha * l + jnp.sum(p, axis=-1, keepdims=True)
    acc = alpha * acc + _dot(p.astype(CDT), v)
    return m_new, l, acc


def _softmax_init(rows, dv):
    return jnp.full((rows, 1), NEG, F32), jnp.zeros((rows, 1), F32), jnp.zeros((rows, dv), F32)


def _prefix_attention(i, tq, logits_fn, v_fn, allowed_fn, o_ref):
    sq = tq // ATT_SUB
    row = lax.broadcasted_iota(I32, (sq, tq), 0)
    col = lax.broadcasted_iota(I32, (sq, tq), 1)

    def step(c, carry, diagonal):
        v = v_fn(c)
        out = []
        for r in range(ATT_SUB):
            s = logits_fn(r, c)
            if diagonal:
                s = jnp.where(allowed_fn(row + r * sq, col), s, NEG)
            out.append(_softmax_step(carry[r], s, v))
        return tuple(out)

    init = tuple(_softmax_init(sq, HEAD_DIM) for _ in range(ATT_SUB))
    carry = lax.fori_loop(0, i, lambda c, cr: step(c, cr, False), init)
    carry = step(i, carry, True)
    for r in range(ATT_SUB):
        m, l, acc = carry[r]
        o_ref[r * sq:(r + 1) * sq, :] = (acc / l).astype(o_ref.dtype)


def _fox_kernel(q_ref, k_ref, v_ref, dq_ref, drow_ref, o_ref, *, tq, scale):
    i = pl.program_id(2)
    sq = tq // ATT_SUB
    qs = [q_ref[0, r * sq:(r + 1) * sq, :] for r in range(ATT_SUB)]
    dqs = [jnp.tile(dq_ref[0, r * sq:(r + 1) * sq, :], (1, tq // LANE)) for r in range(ATT_SUB)]

    def keys(c):
        return pl.ds(pl.multiple_of(c * tq, tq), tq)

    def logits(r, c):
        return _dot_nt(qs[r], k_ref[0, keys(c), :]) * scale + dqs[r] - drow_ref[0, pl.ds(c, 1), :]

    _prefix_attention(i, tq, logits, lambda c: v_ref[0, keys(c), :], lambda row, col: col <= row, o_ref)


def _mla_kernel(qn_ref, qr_ref, kn_ref, kr_ref, v_ref, oin_ref, o_ref, *, tq, scale):
    del oin_ref
    i = pl.program_id(2)
    sq = tq // ATT_SUB
    qns = [qn_ref[0, r * sq:(r + 1) * sq, :] for r in range(ATT_SUB)]
    qrs = [qr_ref[0, r * sq:(r + 1) * sq, :] for r in range(ATT_SUB)]

    def keys(c):
        return pl.ds(pl.multiple_of(c * tq, tq), tq)

    def logits(r, c):
        return (_dot_nt(qns[r], kn_ref[0, keys(c), :]) + _dot_nt(qrs[r], kr_ref[0, keys(c), :])) * scale

    _prefix_attention(i, tq, logits, lambda c: v_ref[0, keys(c), :],
                      lambda row, col: (col >> 6) <= (row >> 6), o_ref)


def fox_call(z1, dq, drow, batch, seq):
    tq = _tile(seq, ATT_TQ)
    nq = seq // tq
    drow = drow.reshape(batch * N_HEADS, nq, tq)
    t = batch * seq
    return pl.pallas_call(
        functools.partial(_fox_kernel, tq=tq, scale=HEAD_DIM ** -0.5),
        grid=(batch, N_HEADS, nq),
        in_specs=[pl.BlockSpec((1, tq, LANE), lambda b, h, i: (h, b * nq + i, 0)),
                  pl.BlockSpec((1, seq, LANE), lambda b, h, i: (HB + h, b, 0)),
                  pl.BlockSpec((1, seq, LANE), lambda b, h, i: (2 * HB + h, b, 0)),
                  pl.BlockSpec((1, tq, LANE), lambda b, h, i: (h, b * nq + i, 0)),
                  pl.BlockSpec((1, nq, tq), lambda b, h, i: (b * N_HEADS + h, 0, 0))],
        out_specs=pl.BlockSpec((tq, LANE), lambda b, h, i: (b * nq + i, h)),
        out_shape=jax.ShapeDtypeStruct((t, 4 * HB * LANE), CDT),
        compiler_params=_cparams("parallel", "parallel", "parallel"),
        name="attn_fox",
    )(z1, z1, z1, dq, drow)


def mla_call(qm, kvb, z3, kr_block, o_all, batch, seq):
    tq = _tile(seq, ATT_TQ)
    nq = seq // tq
    return pl.pallas_call(
        functools.partial(_mla_kernel, tq=tq, scale=(MLA_NOPE + MLA_ROPE) ** -0.5),
        grid=(batch, N_HEADS, nq),
        in_specs=[pl.BlockSpec((1, tq, LANE), lambda b, h, i: (2 * h, b * nq + i, 0)),
                  pl.BlockSpec((1, tq, LANE), lambda b, h, i: (2 * h + 1, b * nq + i, 0)),
                  pl.BlockSpec((1, seq, LANE), lambda b, h, i: (2 * h, b, 0)),
                  pl.BlockSpec((1, seq, LANE), lambda b, h, i: (kr_block, b, 0)),
                  pl.BlockSpec((1, seq, LANE), lambda b, h, i: (2 * h + 1, b, 0)),
                  pl.BlockSpec(memory_space=pl.ANY)],
        out_specs=pl.BlockSpec((tq, LANE), lambda b, h, i: (b * nq + i, HB + h)),
        out_shape=jax.ShapeDtypeStruct(o_all.shape, o_all.dtype),
        input_output_aliases={5: 0},
        compiler_params=_cparams("parallel", "parallel", "parallel"),
        name="attn_mla",
    )(qm, qm, kvb, z3, kvb, o_all)


BAND_TQ = 256
BAND_NKB = (CH_PREV * CHUNK) // BAND_TQ + 1
BIAS_STRIP = 128


def _band_bias_kernel(tab_ref, o_ref):
    h = pl.program_id(0)
    back = (BAND_NKB - 1) * BAND_TQ
    for kb in range(BAND_NKB):
        for q0 in range(0, BAND_TQ, BIAS_STRIP):
            for k0 in range(0, BAND_TQ, BIAS_STRIP):
                q = lax.broadcasted_iota(I32, (BIAS_STRIP, BIAS_STRIP), 0) + q0
                koff = lax.broadcasted_iota(I32, (BIAS_STRIP, BIAS_STRIP), 1) + (kb * BAND_TQ + k0 - back)
                rel = jnp.clip(q - koff, -MAX_REL, MAX_REL) + MAX_REL
                koff0 = kb * BAND_TQ + k0 - back
                lo = int(np.clip(q0 - (koff0 + BIAS_STRIP - 1), -MAX_REL, MAX_REL)) + MAX_REL
                hi = int(np.clip(q0 + BIAS_STRIP - 1 - koff0, -MAX_REL, MAX_REL)) + MAX_REL

                def body(r, acc, rel=rel):
                    return jnp.where(rel == r, tab_ref[h, r], acc)

                acc = lax.fori_loop(lo, hi + 1, body, jnp.zeros((BIAS_STRIP, BIAS_STRIP), F32))
                kc, qc = koff >> 6, q >> 6
                band = (kc >= qc - CH_PREV) & (kc <= qc)
                o_ref[0, kb, q0:q0 + BIAS_STRIP, k0:k0 + BIAS_STRIP] = jnp.where(band, acc, NEG)


def band_bias_call(rel_table):
    return pl.pallas_call(
        _band_bias_kernel,
        grid=(N_HEADS,),
        in_specs=[pl.BlockSpec(memory_space=pltpu.SMEM)],
        out_specs=pl.BlockSpec((1, BAND_NKB, BAND_TQ, BAND_TQ), lambda h: (h, 0, 0, 0)),
        out_shape=jax.ShapeDtypeStruct((N_HEADS, BAND_NKB, BAND_TQ, BAND_TQ), F32),
        compiler_params=_cparams("parallel"),
        name="band_bias",
    )(rel_table.astype(F32))


def _band_kernel(q_ref, k_ref, v_ref, bias_ref, oin_ref, o_ref, *, tq, scale):
    del oin_ref
    i = pl.program_id(1)
    for h in range(N_HEADS):
        q = q_ref[h]
        ss, vs = [], []
        for kb in range(BAND_NKB):
            kt = i - (BAND_NKB - 1) + kb
            ks = pl.ds(pl.multiple_of(jnp.maximum(kt, 0) * tq, tq), tq)
            s = _dot_nt(q, k_ref[h, ks, :]) * scale + bias_ref[h, kb]
            if kb < BAND_NKB - 1:
                s = jnp.where(kt >= 0, s, NEG)
            ss.append(s)
            vs.append(v_ref[h, ks, :])
        m = ss[0].max(axis=-1, keepdims=True)
        for s in ss[1:]:
            m = jnp.maximum(m, s.max(axis=-1, keepdims=True))
        l = jnp.zeros((tq, 1), F32)
        acc = jnp.zeros((tq, HEAD_DIM), F32)
        for s, v in zip(ss, vs):
            p = jnp.exp(s - m)
            l = l + p.sum(axis=-1, keepdims=True)
            acc = acc + _dot(p.astype(CDT), v)
        o_ref[:, h * HEAD_DIM:(h + 1) * HEAD_DIM] = (acc / l).astype(o_ref.dtype)


def band_call(z1, bias_tiles, o_all, batch, seq):
    tq = BAND_TQ
    assert seq % tq == 0
    nq = seq // tq
    return pl.pallas_call(
        functools.partial(_band_kernel, tq=tq, scale=HEAD_DIM ** -0.5),
        grid=(batch, nq),
        in_specs=[pl.BlockSpec((HB, tq, LANE), lambda b, i: (3, b * nq + i, 0)),
                  pl.BlockSpec((HB, seq, LANE), lambda b, i: (4, b, 0)),
                  pl.BlockSpec((HB, seq, LANE), lambda b, i: (5, b, 0)),
                  pl.BlockSpec(bias_tiles.shape, lambda b, i: (0, 0, 0, 0)),
                  pl.BlockSpec(memory_space=pl.ANY)],
        out_specs=pl.BlockSpec((tq, HB * LANE), lambda b, i: (b * nq + i, 2)),
        out_shape=jax.ShapeDtypeStruct(o_all.shape, o_all.dtype),
        input_output_aliases={4: 0},
        compiler_params=_cparams("parallel", "parallel"),
        name="attn_band",
    )(z1, z1, z1, bias_tiles, o_all)


DSA_TQ = 128
DSA_KSTEP = 512
DSA_HGROUP = 4


def _count(mask):
    return jnp.sum(mask.astype(F32), axis=-1, keepdims=True)


def _dsa_body(qd_ref, kd_ref, vd_ref, qi_ref, ka_ref, kb_ref, wi_ref, o_ref, *, i, tq, ext, k_sel, scale):
    assert IDX_DIM == 64
    w = (wi_ref[...] * (IDX_HEADS ** -0.5)) * (IDX_DIM ** -0.5)
    ka, kb = ka_ref[0, :ext, :], kb_ref[0, :ext, :]
    score = jnp.zeros((tq, ext), F32)
    for p in range(IDX_HEADS // 2):
        qp = qi_ref[p]
        l0 = _dot_nt(qp, ka)
        l1 = _dot_nt(qp, kb)
        c0 = N_HEADS + 2 * p
        score = score + jnp.maximum(l0, 0.0) * w[:, c0:c0 + 1] + jnp.maximum(l1, 0.0) * w[:, c0 + 1:c0 + 2]
    score = score + 0.0
    col = lax.broadcasted_iota(I32, (tq, ext), 1)
    tpos = lax.broadcasted_iota(I32, (tq, 1), 0) + i * tq
    limit = ((tpos >> 6) + 1) << 6
    valid = col < limit
    bits = lax.bitcast_convert_type(score, I32)
    key = jnp.where(bits < 0, bits ^ 0x7FFFFFFF, bits)
    key = jnp.where(valid, key, INT_MIN)

    def thr_body(it, tu):
        cand = tu | lax.shift_left(jnp.int32(1), 31 - it)
        cnt = _count(key >= (cand ^ INT_MIN))
        return jnp.where(cnt >= k_sel, cand, tu)

    tu = lax.fori_loop(0, 32, thr_body, jnp.zeros((tq, 1), I32))
    thr = tu ^ INT_MIN
    gt = key > thr
    ties = (key == thr) & valid
    n_gt = _count(gt)
    need = k_sel - n_gt
    excess = jnp.max(n_gt + _count(ties)) > k_sel

    nbits = max(1, int(np.ceil(np.log2(ext))))

    def tie_search():
        def tie_body(it, j):
            cand = j | lax.shift_left(jnp.int32(1), nbits - 1 - it)
            cnt = _count(ties & (col < cand))
            return jnp.where(cnt < need, cand, j)

        return lax.fori_loop(0, nbits, tie_body, jnp.zeros((tq, 1), I32))

    j = lax.cond(excess, tie_search, lambda: jnp.full((tq, 1), ext, I32))
    sel = gt | (ties & (col <= j))
    maskb = jnp.where(sel, 0.0, NEG)

    kd, vd = kd_ref[0, :ext, :], vd_ref[0, :ext, :]
    for g in range(N_HEADS // DSA_HGROUP):
        q = qd_ref[g * DSA_HGROUP:(g + 1) * DSA_HGROUP].reshape(DSA_HGROUP * tq, HEAD_DIM)
        s = _dot_nt(q, kd) * scale + jnp.tile(maskb, (DSA_HGROUP, 1))
        m = jnp.max(s, axis=-1, keepdims=True)
        p = jnp.exp(s - m)
        l = jnp.sum(p, axis=-1, keepdims=True)
        o = _dot(p.astype(CDT), vd) / l
        for hh in range(DSA_HGROUP):
            h = g * DSA_HGROUP + hh
            o_ref[:, h * HEAD_DIM:(h + 1) * HEAD_DIM] = o[hh * tq:(hh + 1) * tq, :].astype(o_ref.dtype)


def _dsa_kernel(qd_ref, kd_ref, vd_ref, qi_ref, ka_ref, kb_ref, wi_ref, oin_ref, o_ref, *, tq, seq, kstep, k_sel, scale):
    del oin_ref
    i = pl.program_id(1)
    nsteps = ((i + 1) * tq - 1) // kstep + 1
    for n in range(1, seq // kstep + 1):
        @pl.when(nsteps == n)
        def _(n=n):
            _dsa_body(qd_ref, kd_ref, vd_ref, qi_ref, ka_ref, kb_ref, wi_ref, o_ref,
                      i=i, tq=tq, ext=n * kstep, k_sel=k_sel, scale=scale)


def dsa_call(z2, z3, tail, o_all, batch, seq):
    tq = _tile(seq, DSA_TQ)
    nq = seq // tq
    kstep = _tile(seq, DSA_KSTEP)
    k_sel = min(TOPK_MAX, seq // 4)
    return pl.pallas_call(
        functools.partial(_dsa_kernel, tq=tq, seq=seq, kstep=kstep, k_sel=k_sel, scale=HEAD_DIM ** -0.5),
        grid=(batch, nq),
        in_specs=[pl.BlockSpec((HB, tq, LANE), lambda b, i: (0, b * nq + i, 0)),
                  pl.BlockSpec((1, seq, LANE), lambda b, i: (HB, b, 0)),
                  pl.BlockSpec((1, seq, LANE), lambda b, i: (HB + 1, b, 0)),
                  pl.BlockSpec((HB, tq, LANE), lambda b, i: (0, b * nq + i, 0)),
                  pl.BlockSpec((1, seq, LANE), lambda b, i: (HB, b, 0)),
                  pl.BlockSpec((1, seq, LANE), lambda b, i: (HB + 1, b, 0)),
                  pl.BlockSpec((tq, LANE), lambda b, i: (b * nq + i, 0)),
                  pl.BlockSpec(memory_space=pl.ANY)],
        out_specs=pl.BlockSpec((tq, HB * LANE), lambda b, i: (b * nq + i, 3)),
        out_shape=jax.ShapeDtypeStruct(o_all.shape, o_all.dtype),
        input_output_aliases={7: 0},
        compiler_params=_cparams("parallel", "parallel"),
        name="attn_dsa",
    )(z2, z2, z2, z3, z3, z3, tail, o_all)


def _merge_kernel(u_ref, o_ref, wg_ref, bg_ref, wb_ref, out_ref, acc_ref, *, nbranch):
    n = pl.program_id(2)
    g = _dot(u_ref[...], wg_ref[0]) + bg_ref[0]
    term = _sigmoid(g) * _dot(o_ref[...], wb_ref[0])

    @pl.when(n == 0)
    def _():
        acc_ref[...] = term

    @pl.when(n > 0)
    def _():
        acc_ref[...] += term

    @pl.when(n == nbranch - 1)
    def _():
        out_ref[...] = acc_ref[...].astype(out_ref.dtype)


def merge_call(u, o_all, wg, bg, wb):
    t, d = u.shape
    nbranch, bw = wb.shape[0], wb.shape[1]
    tm, tn = _tile(t, 1024), _tile(d, 512)
    return pl.pallas_call(
        functools.partial(_merge_kernel, nbranch=nbranch),
        grid=(t // tm, d // tn, nbranch),
        in_specs=[pl.BlockSpec((tm, d), lambda i, j, n: (i, 0)),
                  pl.BlockSpec((tm, bw), lambda i, j, n: (i, n)),
                  pl.BlockSpec((1, d, tn), lambda i, j, n: (n, 0, j)),
                  pl.BlockSpec((1, 1, tn), lambda i, j, n: (n, 0, j)),
                  pl.BlockSpec((1, bw, tn), lambda i, j, n: (n, 0, j))],
        out_specs=pl.BlockSpec((tm, tn), lambda i, j, n: (i, j)),
        out_shape=jax.ShapeDtypeStruct((t, d), CDT),
        scratch_shapes=[pltpu.VMEM((tm, tn), F32)],
        compiler_params=_cparams("parallel", "parallel", "arbitrary"),
        name="gated_merge",
    )(u, o_all, wg, bg, wb)


def _pack_w_in(w_in):
    off = np.concatenate([[0], np.cumsum(SPLIT_SIZES)])
    col = {n: w_in[:, off[k]:off[k + 1]] for k, n in enumerate(SPLIT_NAMES)}
    d = w_in.shape[0]
    z64 = jnp.zeros((d, 64), w_in.dtype)
    w1 = jnp.concatenate([col[n] for n in ("qa", "ka", "va", "qc", "kc", "vc")], axis=1)
    w2 = jnp.concatenate([col["qd"], col["kd"], col["vd"]], axis=1)
    w3 = jnp.concatenate([col["qi"], col["ki"], z64, z64, col["ki"], col["kr"], z64], axis=1)
    w4 = jnp.concatenate([col["cq"], col["ckv"]], axis=1)
    w5 = jnp.concatenate([col["fa"], col["wi"], jnp.zeros((d, LANE - 24), w_in.dtype)], axis=1)
    w5t = jnp.concatenate([w5[:, :24], jnp.zeros((d, TAIL_ROWS - 24), w_in.dtype)], axis=1).T
    return [w.astype(CDT) for w in (w1, w2, w3, w4, w5, w5t)]


def _pack_w_uq(w_uq):
    r = w_uq.shape[0]
    w = w_uq.reshape(r, N_HEADS, MLA_NOPE + MLA_ROPE)
    w = jnp.concatenate([w, jnp.zeros((r, N_HEADS, 2 * LANE - MLA_NOPE - MLA_ROPE), w_uq.dtype)], axis=-1)
    return w.reshape(r, N_HEADS * 2 * LANE).astype(CDT)


def kernel(x, p, w_in, b_f, g_cq, g_ckv, w_uq, w_ukv, rel_bias, w_branch, w_gate, b_gate, w_out, w1_gate, w1_up, w1_down, w2_gate, w2_up, w2_down, g_ffn1_pre, g_ffn1_post, g_mix_pre, g_mix_post, g_ffn2_pre, g_ffn2_post, g_ple_pre, g_ple_post, w_ple, w_ple_gate):
    batch, seq, d = x.shape
    depth = w_in.shape[0]
    t = batch * seq
    cos128, sin128 = _rope_tables(seq, HEAD_DIM // 2)
    cos64, sin64 = _rope_tables(seq, IDX_DIM // 2)

    h = x.reshape(t, d)
    u = rmsnorm_call(h, g_ffn1_pre[0])
    for i in range(depth):
        a = swiglu_call(u, w1_gate[i].astype(CDT), w1_up[i].astype(CDT))
        h, u = rowres_call(a, w1_down[i].astype(CDT), h, g_ffn1_post[i], g_mix_pre[i], 0.5)

        w1, w2, w3, w4, w5, w5t = _pack_w_in(w_in[i])
        z1 = mm_call(u, w1)
        z2 = proj_rope_call(u, w2, cos128, sin128, seq, HEAD_DIM // 2, range(HB + 1))
        z3 = proj_rope_call(u, w3, cos64, sin64, seq, IDX_DIM // 2, range(HB + 3))
        qm, kvb = proj_mla_call(u, w4, g_cq[i], g_ckv[i], _pack_w_uq(w_uq[i]), w_ukv[i].astype(CDT), cos64, sin64, seq)
        tail, tailt = proj_tail_call(u, w5, w5t, batch, seq)
        dq, drow = decay_call(tail, tailt, b_f[i], batch, seq)
        o_all = fox_call(z1, dq, drow, batch, seq)
        o_all = mla_call(qm, kvb, z3, HB + 2, o_all, batch, seq)
        o_all = band_call(z1, band_bias_call(rel_bias[i]), o_all, batch, seq)
        o_all = dsa_call(z2, z3, tail, o_all, batch, seq)
        merged = merge_call(u, o_all, w_gate[i].astype(CDT), b_gate[i].reshape(-1, 1, d), w_branch[i].astype(CDT))
        h, u = rowres_call(merged, w_out[i].astype(CDT), h, g_mix_post[i], g_ffn2_pre[i], 1.0)

        a = swiglu_call(u, w2_gate[i].astype(CDT), w2_up[i].astype(CDT))
        h, u = rowres_call(a, w2_down[i].astype(CDT), h, g_ffn2_post[i], g_ple_pre[i], 0.5)

        last = i == depth - 1
        g_next = g_ple_pre[i] if last else g_ffn1_pre[i + 1]
        h, u = rowres_call(u, w_ple_gate[i].astype(CDT), h, g_ple_post[i], g_next, 1.0,
                           ple=(p[i].reshape(t, -1), w_ple[i].astype(CDT)), emit_u=not last)
    return h.reshape(batch, seq, d)
```

```python
import functools

import numpy as np
import jax
import jax.numpy as jnp
from jax import lax
from jax.experimental import pallas as pl
from jax.experimental.pallas import tpu as pltpu

F32 = jnp.float32
I32 = jnp.int32
CDT = jnp.bfloat16
EPS = 1e-6
NEG = -1e30
ROPE_THETA = 10000.0

LANE = 128
HEAD_DIM = 128
N_HEADS = 8
HB = N_HEADS * HEAD_DIM // LANE
CHUNK = 64
MLA_Q_LORA, MLA_KV_LORA, MLA_NOPE, MLA_ROPE = 768, 512, 128, 64
CH_PREV, MAX_REL = 8, 128
IDX_HEADS, IDX_DIM = 16, 64
TOPK_MAX = 256
SPLIT_SIZES = (1024, 1024, 1024, 8, 768, 512, 64, 1024, 1024, 1024, 1024, 128, 128, 1024, 64, 16)
SPLIT_NAMES = ("qa", "ka", "va", "fa", "cq", "ckv", "kr", "qc", "kc", "vc", "qd", "kd", "vd", "qi", "ki", "wi")
INT_MIN = -2147483648

VMEM_LIMIT_BYTES = 60 * 1024 * 1024


def _cparams(*sem):
    return pltpu.CompilerParams(dimension_semantics=sem, vmem_limit_bytes=VMEM_LIMIT_BYTES)


def _tile(n, pref):
    t = min(n, pref)
    assert n % t == 0, (n, pref)
    return t


def _dot(a, b):
    return jnp.dot(a, b, preferred_element_type=F32)


def _dot_nt(a, b):
    return lax.dot_general(a, b, (((1,), (1,)), ((), ())), preferred_element_type=F32)


def _rms(y, g):
    ms = jnp.mean(y * y, axis=-1, keepdims=True)
    return y * lax.rsqrt(ms + EPS) * g


def _sigmoid(x):
    return 1.0 / (1.0 + jnp.exp(-x))


def _rmsnorm_kernel(x_ref, g_ref, o_ref):
    o_ref[...] = _rms(x_ref[...], g_ref[...]).astype(o_ref.dtype)


def rmsnorm_call(x, g):
    t, d = x.shape
    tm = _tile(t, 256)
    return pl.pallas_call(
        _rmsnorm_kernel,
        grid=(t // tm,),
        in_specs=[pl.BlockSpec((tm, d), lambda i: (i, 0)), pl.BlockSpec((1, d), lambda i: (0, 0))],
        out_specs=pl.BlockSpec((tm, d), lambda i: (i, 0)),
        out_shape=jax.ShapeDtypeStruct((t, d), CDT),
        compiler_params=_cparams("parallel"),
        name="rmsnorm",
    )(x, g.reshape(1, d))


def _swiglu_kernel(a_ref, wg_ref, wu_ref, o_ref):
    a = a_ref[...]
    g = _dot(a, wg_ref[...])
    v = _dot(a, wu_ref[...])
    o_ref[...] = (g * _sigmoid(g) * v).astype(o_ref.dtype)


def swiglu_call(u, wg, wu):
    t, d = u.shape
    f = wg.shape[1]
    tm, tn = _tile(t, 1024), _tile(f, 512)
    return pl.pallas_call(
        _swiglu_kernel,
        grid=(t // tm, f // tn),
        in_specs=[pl.BlockSpec((tm, d), lambda i, j: (i, 0)),
                  pl.BlockSpec((d, tn), lambda i, j: (0, j)),
                  pl.BlockSpec((d, tn), lambda i, j: (0, j))],
        out_specs=pl.BlockSpec((tm, tn), lambda i, j: (i, j)),
        out_shape=jax.ShapeDtypeStruct((t, f), CDT),
        compiler_params=_cparams("parallel", "parallel"),
        name="swiglu_up",
    )(u, wg, wu)


ROW_CHUNK = 64
ROWRES_TK = 4096


def _rowres_kernel(*refs, coef, nk, nj, tn, ple, emit_u):
    a_ref, w_ref, h_ref, gp_ref, gn_ref = refs[:5]
    pos = 5
    if ple:
        p_ref, wple_ref = refs[pos:pos + 2]
        pos += 2
    hout_ref = refs[pos]
    uout_ref = refs[pos + 1] if emit_u else None
    k, j = pl.program_id(1), pl.program_id(2)
    cols = pl.ds(pl.multiple_of(j * tn, tn), tn)
    part = _dot(a_ref[...], w_ref[...])
    if nk == 1:
        hout_ref[:, cols] = part
    else:
        @pl.when(k == 0)
        def _():
            hout_ref[:, cols] = part

        @pl.when(k > 0)
        def _():
            hout_ref[:, cols] += part

    @pl.when((k == nk - 1) & (j == nj - 1))
    def _():
        tm = hout_ref.shape[0]
        rc = min(ROW_CHUNK, tm)

        def body(r, carry):
            rows = pl.ds(pl.multiple_of(r * rc, rc), rc)
            y = hout_ref[rows, :]
            if ple:
                e = _dot(p_ref[rows, :].astype(CDT), wple_ref[...])
                y = _sigmoid(y) * e
            hn = h_ref[rows, :] + coef * _rms(y, gp_ref[...])
            hout_ref[rows, :] = hn
            if emit_u:
                uout_ref[rows, :] = _rms(hn, gn_ref[...]).astype(uout_ref.dtype)
            return carry

        lax.fori_loop(0, tm // rc, body, 0)


def rowres_call(a, w, h, g_post, g_next, coef, ple=None, emit_u=True):
    t, kdim = a.shape
    d = w.shape[1]
    tm, tk, tn = _tile(t, 512), _tile(kdim, ROWRES_TK), _tile(d, 512)
    nk, nj = kdim // tk, d // tn
    once = pl.Buffered(1)
    w = w.reshape(nk, tk, nj, tn).transpose(0, 2, 1, 3)
    in_specs = [pl.BlockSpec((tm, tk), lambda i, k, j: (i, k)),
                pl.BlockSpec((None, None, tk, tn), lambda i, k, j: (k, j, 0, 0)),
                pl.BlockSpec((tm, d), lambda i, k, j: (i, 0), pipeline_mode=once),
                pl.BlockSpec((1, d), lambda i, k, j: (0, 0), pipeline_mode=once),
                pl.BlockSpec((1, d), lambda i, k, j: (0, 0), pipeline_mode=once)]
    args = [a, w, h, g_post.reshape(1, d), g_next.reshape(1, d)]
    if ple is not None:
        p, wple = ple
        in_specs += [pl.BlockSpec((tm, p.shape[1]), lambda i, k, j: (i, 0), pipeline_mode=once),
                     pl.BlockSpec(wple.shape, lambda i, k, j: (0, 0), pipeline_mode=once)]
        args += [p, wple]
    out_specs = [pl.BlockSpec((tm, d), lambda i, k, j: (i, 0))]
    out_shape = [jax.ShapeDtypeStruct((t, d), F32)]
    if emit_u:
        out_specs.append(pl.BlockSpec((tm, d), lambda i, k, j: (i, 0)))
        out_shape.append(jax.ShapeDtypeStruct((t, d), CDT))
    res = pl.pallas_call(
        functools.partial(_rowres_kernel, coef=coef, nk=nk, nj=nj, tn=tn, ple=ple is not None, emit_u=emit_u),
        grid=(t // tm, nk, nj),
        in_specs=in_specs,
        out_specs=out_specs,
        out_shape=out_shape,
        compiler_params=_cparams("parallel", "arbitrary", "arbitrary"),
        name="rowres_ple" if ple is not None else "rowres",
    )(*args)
    return (res[0], res[1]) if emit_u else (res[0], None)


def _mm_kernel(a_ref, w_ref, o_ref):
    acc = _dot(a_ref[...], w_ref[...])
    for blk in range(o_ref.shape[0]):
        o_ref[blk] = acc[:, blk * LANE:(blk + 1) * LANE].astype(o_ref.dtype)


def mm_call(a, w):
    t, d = a.shape
    n = w.shape[1]
    tm, tn = _tile(t, 1024), _tile(n, 1024)
    return pl.pallas_call(
        _mm_kernel,
        grid=(t // tm, n // tn),
        in_specs=[pl.BlockSpec((tm, d), lambda i, j: (i, 0)), pl.BlockSpec((d, tn), lambda i, j: (0, j))],
        out_specs=pl.BlockSpec((tn // LANE, tm, LANE), lambda i, j: (j, i, 0)),
        out_shape=jax.ShapeDtypeStruct((n // LANE, t, LANE), CDT),
        compiler_params=_cparams("parallel", "parallel"),
        name="proj_plain",
    )(a, w)


def _rope_tables(seq, half):
    inv = ROPE_THETA ** (-jnp.arange(half, dtype=F32) / half)
    ang = jnp.arange(seq, dtype=F32)[:, None] * inv[None, :]
    cos, sin = jnp.cos(ang), jnp.sin(ang)
    reps = LANE // (2 * half)
    return jnp.tile(jnp.concatenate([cos, cos], -1), (1, reps)), jnp.tile(jnp.concatenate([-sin, sin], -1), (1, reps))


def _rope_block(x, c, s, half):
    if 2 * half == LANE:
        sw = pltpu.roll(x, half, axis=1)
    else:
        lane = lax.broadcasted_iota(I32, x.shape, 1)
        first = (lane & (2 * half - 1)) < half
        sw = jnp.where(first, pltpu.roll(x, LANE - half, axis=1), pltpu.roll(x, half, axis=1))
    return x * c + sw * s


def _proj_rope_kernel(a_ref, w_ref, c_ref, s_ref, o_ref, *, half, rope_blocks):
    acc = _dot(a_ref[...], w_ref[...])
    c, s = c_ref[...], s_ref[...]
    for blk in range(o_ref.shape[0]):
        x = acc[:, blk * LANE:(blk + 1) * LANE]
        if blk in rope_blocks:
            x = _rope_block(x, c, s, half)
        o_ref[blk] = x.astype(o_ref.dtype)


def proj_rope_call(u, w, cos, sin, seq, half, rope_blocks):
    t, d = u.shape
    n = w.shape[1]
    tm = _tile(seq, 512)
    ns = seq // tm
    return pl.pallas_call(
        functools.partial(_proj_rope_kernel, half=half, rope_blocks=tuple(rope_blocks)),
        grid=(t // tm,),
        in_specs=[pl.BlockSpec((tm, d), lambda i: (i, 0)),
                  pl.BlockSpec((d, n), lambda i: (0, 0)),
                  pl.BlockSpec((tm, LANE), lambda i: (i % ns, 0)),
                  pl.BlockSpec((tm, LANE), lambda i: (i % ns, 0))],
        out_specs=pl.BlockSpec((n // LANE, tm, LANE), lambda i: (0, i, 0)),
        out_shape=jax.ShapeDtypeStruct((n // LANE, t, LANE), CDT),
        compiler_params=_cparams("parallel"),
        name=f"proj_rope{2 * half}",
    )(u, w, cos, sin)


def _proj_mla_kernel(a_ref, w_ref, gq_ref, gkv_ref, wuq_ref, wukv_ref, c_ref, s_ref, q_ref, kv_ref):
    acc = _dot(a_ref[...], w_ref[...])
    cq = _rms(acc[:, :MLA_Q_LORA], gq_ref[...]).astype(CDT)
    ckv = _rms(acc[:, MLA_Q_LORA:MLA_Q_LORA + MLA_KV_LORA], gkv_ref[...]).astype(CDT)
    qb = _dot(cq, wuq_ref[...])
    c, s = c_ref[...], s_ref[...]
    for blk in range(q_ref.shape[0]):
        x = qb[:, blk * LANE:(blk + 1) * LANE]
        if blk % 2 == 1:
            x = _rope_block(x, c, s, MLA_ROPE // 2)
        q_ref[blk] = x.astype(q_ref.dtype)
    kvb = _dot(ckv, wukv_ref[...])
    for blk in range(kv_ref.shape[0]):
        kv_ref[blk] = kvb[:, blk * LANE:(blk + 1) * LANE].astype(kv_ref.dtype)


def proj_mla_call(u, w, gq, gkv, wuq, wukv, cos, sin, seq):
    t, d = u.shape
    n = w.shape[1]
    tm = _tile(seq, 512)
    ns = seq // tm
    nq, nkv = wuq.shape[1] // LANE, wukv.shape[1] // LANE
    return pl.pallas_call(
        _proj_mla_kernel,
        grid=(t // tm,),
        in_specs=[pl.BlockSpec((tm, d), lambda i: (i, 0)),
                  pl.BlockSpec((d, n), lambda i: (0, 0)),
                  pl.BlockSpec((1, MLA_Q_LORA), lambda i: (0, 0)),
                  pl.BlockSpec((1, MLA_KV_LORA), lambda i: (0, 0)),
                  pl.BlockSpec(wuq.shape, lambda i: (0, 0)),
                  pl.BlockSpec(wukv.shape, lambda i: (0, 0)),
                  pl.BlockSpec((tm, LANE), lambda i: (i % ns, 0)),
                  pl.BlockSpec((tm, LANE), lambda i: (i % ns, 0))],
        out_specs=[pl.BlockSpec((nq, tm, LANE), lambda i: (0, i, 0)), pl.BlockSpec((nkv, tm, LANE), lambda i: (0, i, 0))],
        out_shape=[jax.ShapeDtypeStruct((nq, t, LANE), CDT), jax.ShapeDtypeStruct((nkv, t, LANE), CDT)],
        compiler_params=_cparams("parallel"),
        name="proj_mla",
    )(u, w, gq.reshape(1, -1), gkv.reshape(1, -1), wuq, wukv, cos, sin)


TAIL_ROWS = 32


def _proj_tail_kernel(a_ref, w_ref, wt_ref, o_ref, ot_ref):
    a = a_ref[...]
    o_ref[...] = _dot(a, w_ref[...])
    ot_ref[0] = _dot_nt(wt_ref[...], a)


def proj_tail_call(u, w, wt, batch, seq):
    t, d = u.shape
    tm = _tile(seq, 512)
    ns = seq // tm
    return pl.pallas_call(
        _proj_tail_kernel,
        grid=(t // tm,),
        in_specs=[pl.BlockSpec((tm, d), lambda i: (i, 0)),
                  pl.BlockSpec((d, LANE), lambda i: (0, 0)),
                  pl.BlockSpec((TAIL_ROWS, d), lambda i: (0, 0))],
        out_specs=[pl.BlockSpec((tm, LANE), lambda i: (i, 0)),
                   pl.BlockSpec((1, TAIL_ROWS, tm), lambda i: (i // ns, 0, i % ns))],
        out_shape=[jax.ShapeDtypeStruct((t, LANE), F32), jax.ShapeDtypeStruct((batch, TAIL_ROWS, seq), F32)],
        compiler_params=_cparams("parallel"),
        name="proj_tail",
    )(u, w, wt)


def _log_sigmoid(x):
    return jnp.minimum(x, 0.0) - jnp.log1p(jnp.exp(-jnp.abs(x)))


def _decay_kernel(tailt_ref, brow_ref, drow_ref):
    y = _log_sigmoid(tailt_ref[0, :N_HEADS, :] + brow_ref[...])
    seq = y.shape[1]
    idy = lax.broadcasted_iota(I32, y.shape, 1)
    d = 1
    while d < seq:
        y = y + jnp.where(idy >= d, pltpu.roll(y, d, axis=1), 0.0)
        d *= 2
    drow_ref[0] = y


def decay_call(tailt, b_f, batch, seq):
    return pl.pallas_call(
        _decay_kernel,
        grid=(batch,),
        in_specs=[pl.BlockSpec((1, TAIL_ROWS, seq), lambda b: (b, 0, 0)),
                  pl.BlockSpec((N_HEADS, 1), lambda b: (0, 0))],
        out_specs=pl.BlockSpec((1, N_HEADS, seq), lambda b: (b, 0, 0)),
        out_shape=jax.ShapeDtypeStruct((batch, N_HEADS, seq), F32),
        compiler_params=_cparams("parallel"),
        name="decay_scan",
    )(tailt, b_f.reshape(N_HEADS, 1))


ATT_TQ = 512
ATT_SUB = 2


def _softmax_step(carry, s, v):
    m, l, acc = carry
    m_new = jnp.maximum(m, jnp.max(s, axis=-1, keepdims=True))
    alpha = jnp.exp(m - m_new)
    p = jnp.exp(s - m_new)
    l = alpha * l + jnp.sum(p, axis=-1, keepdims=True)
    acc = alpha * acc + _dot(p.astype(CDT), v)
    return m_new, l, acc


def _softmax_init(rows, dv):
    return jnp.full((rows, 1), NEG, F32), jnp.zeros((rows, 1), F32), jnp.zeros((rows, dv), F32)


def _prefix_attention(i, tq, logits_fn, v_fn, allowed_fn, o_ref):
    sq = tq // ATT_SUB

    def step(c, carry, diagonal):
        out = []
        for r in range(ATT_SUB):
            n = (r + 1) * sq if diagonal else tq
            s = logits_fn(r, c, n)
            if diagonal:
                row = lax.broadcasted_iota(I32, (sq, n), 0) + r * sq
                col = lax.broadcasted_iota(I32, (sq, n), 1)
                s = jnp.where(allowed_fn(row, col), s, NEG)
            out.append(_softmax_step(carry[r], s, v_fn(c, n)))
        return tuple(out)

    init = tuple(_softmax_init(sq, HEAD_DIM) for _ in range(ATT_SUB))
    carry = lax.fori_loop(0, i, lambda c, cr: step(c, cr, False), init)
    carry = step(i, carry, True)
    for r in range(ATT_SUB):
        m, l, acc = carry[r]
        o_ref[r * sq:(r + 1) * sq, :] = (acc / l).astype(o_ref.dtype)


def _fox_kernel(q_ref, k_ref, v_ref, drow_ref, o_ref, *, tq, scale):
    i = pl.program_id(2)
    sq = tq // ATT_SUB
    qs = [q_ref[0, r * sq:(r + 1) * sq, :] for r in range(ATT_SUB)]

    def keys(c, n):
        return pl.ds(pl.multiple_of(c * tq, tq), n)

    def logits(r, c, n):
        return _dot_nt(qs[r], k_ref[0, keys(c, n), :]) * scale - drow_ref[0, pl.ds(c, 1), :n]

    _prefix_attention(i, tq, logits, lambda c, n: v_ref[0, keys(c, n), :], lambda row, col: col <= row, o_ref)


def _mla_kernel(qn_ref, qr_ref, kn_ref, kr_ref, v_ref, oin_ref, o_ref, *, tq, scale):
    del oin_ref
    i = pl.program_id(2)
    sq = tq // ATT_SUB
    qns = [qn_ref[0, r * sq:(r + 1) * sq, :] for r in range(ATT_SUB)]
    qrs = [qr_ref[0, r * sq:(r + 1) * sq, :] for r in range(ATT_SUB)]

    def keys(c, n):
        return pl.ds(pl.multiple_of(c * tq, tq), n)

    def logits(r, c, n):
        return (_dot_nt(qns[r], kn_ref[0, keys(c, n), :]) + _dot_nt(qrs[r], kr_ref[0, keys(c, n), :])) * scale

    _prefix_attention(i, tq, logits, lambda c, n: v_ref[0, keys(c, n), :],
                      lambda row, col: (col >> 6) <= (row >> 6), o_ref)


def fox_call(z1, drow, batch, seq):
    tq = _tile(seq, ATT_TQ)
    nq = seq // tq
    drow = drow.reshape(batch * N_HEADS, nq, tq)
    t = batch * seq
    return pl.pallas_call(
        functools.partial(_fox_kernel, tq=tq, scale=HEAD_DIM ** -0.5),
        grid=(batch, N_HEADS, nq),
        in_specs=[pl.BlockSpec((1, tq, LANE), lambda b, h, i: (h, b * nq + i, 0)),
                  pl.BlockSpec((1, seq, LANE), lambda b, h, i: (HB + h, b, 0)),
                  pl.BlockSpec((1, seq, LANE), lambda b, h, i: (2 * HB + h, b, 0)),
                  pl.BlockSpec((1, nq, tq), lambda b, h, i: (b * N_HEADS + h, 0, 0))],
        out_specs=pl.BlockSpec((tq, LANE), lambda b, h, i: (b * nq + i, h)),
        out_shape=jax.ShapeDtypeStruct((t, 4 * HB * LANE), CDT),
        compiler_params=_cparams("parallel", "parallel", "parallel"),
        name="attn_fox",
    )(z1, z1, z1, drow)


def mla_call(qm, kvb, z3, kr_block, o_all, batch, seq):
    tq = _tile(seq, ATT_TQ)
    nq = seq // tq
    return pl.pallas_call(
        functools.partial(_mla_kernel, tq=tq, scale=(MLA_NOPE + MLA_ROPE) ** -0.5),
        grid=(batch, N_HEADS, nq),
        in_specs=[pl.BlockSpec((1, tq, LANE), lambda b, h, i: (2 * h, b * nq + i, 0)),
                  pl.BlockSpec((1, tq, LANE), lambda b, h, i: (2 * h + 1, b * nq + i, 0)),
                  pl.BlockSpec((1, seq, LANE), lambda b, h, i: (2 * h, b, 0)),
                  pl.BlockSpec((1, seq, LANE), lambda b, h, i: (kr_block, b, 0)),
                  pl.BlockSpec((1, seq, LANE), lambda b, h, i: (2 * h + 1, b, 0)),
                  pl.BlockSpec(memory_space=pl.ANY)],
        out_specs=pl.BlockSpec((tq, LANE), lambda b, h, i: (b * nq + i, HB + h)),
        out_shape=jax.ShapeDtypeStruct(o_all.shape, o_all.dtype),
        input_output_aliases={5: 0},
        compiler_params=_cparams("parallel", "parallel", "parallel"),
        name="attn_mla",
    )(qm, qm, kvb, z3, kvb, o_all)


BAND_TQ = 256
BAND_NKB = (CH_PREV * CHUNK) // BAND_TQ + 1
BIAS_STRIP = 128


def _band_bias_kernel(tab_ref, o_ref):
    h = pl.program_id(0)
    back = (BAND_NKB - 1) * BAND_TQ
    for kb in range(BAND_NKB):
        for q0 in range(0, BAND_TQ, BIAS_STRIP):
            for k0 in range(0, BAND_TQ, BIAS_STRIP):
                q = lax.broadcasted_iota(I32, (BIAS_STRIP, BIAS_STRIP), 0) + q0
                koff = lax.broadcasted_iota(I32, (BIAS_STRIP, BIAS_STRIP), 1) + (kb * BAND_TQ + k0 - back)
                rel = jnp.clip(q - koff, -MAX_REL, MAX_REL) + MAX_REL
                koff0 = kb * BAND_TQ + k0 - back
                lo = int(np.clip(q0 - (koff0 + BIAS_STRIP - 1), -MAX_REL, MAX_REL)) + MAX_REL
                hi = int(np.clip(q0 + BIAS_STRIP - 1 - koff0, -MAX_REL, MAX_REL)) + MAX_REL

                def body(r, acc, rel=rel):
                    return jnp.where(rel == r, tab_ref[h, r], acc)

                acc = lax.fori_loop(lo, hi + 1, body, jnp.zeros((BIAS_STRIP, BIAS_STRIP), F32))
                kc, qc = koff >> 6, q >> 6
                band = (kc >= qc - CH_PREV) & (kc <= qc)
                o_ref[0, kb, q0:q0 + BIAS_STRIP, k0:k0 + BIAS_STRIP] = jnp.where(band, acc, NEG)


def band_bias_call(rel_table):
    return pl.pallas_call(
        _band_bias_kernel,
        grid=(N_HEADS,),
        in_specs=[pl.BlockSpec(memory_space=pltpu.SMEM)],
        out_specs=pl.BlockSpec((1, BAND_NKB, BAND_TQ, BAND_TQ), lambda h: (h, 0, 0, 0)),
        out_shape=jax.ShapeDtypeStruct((N_HEADS, BAND_NKB, BAND_TQ, BAND_TQ), F32),
        compiler_params=_cparams("parallel"),
        name="band_bias",
    )(rel_table.astype(F32))


def _band_kernel(q_ref, k_ref, v_ref, bias_ref, oin_ref, o_ref, *, tq, scale):
    del oin_ref
    i = pl.program_id(1)
    for h in range(N_HEADS):
        q = q_ref[h]
        ss, vs = [], []
        for kb in range(BAND_NKB):
            kt = i - (BAND_NKB - 1) + kb
            ks = pl.ds(pl.multiple_of(jnp.maximum(kt, 0) * tq, tq), tq)
            s = _dot_nt(q, k_ref[h, ks, :]) * scale + bias_ref[h, kb]
            if kb < BAND_NKB - 1:
                s = jnp.where(kt >= 0, s, NEG)
            ss.append(s)
            vs.append(v_ref[h, ks, :])
        m = ss[0].max(axis=-1, keepdims=True)
        for s in ss[1:]:
            m = jnp.maximum(m, s.max(axis=-1, keepdims=True))
        l = jnp.zeros((tq, 1), F32)
        acc = jnp.zeros((tq, HEAD_DIM), F32)
        for s, v in zip(ss, vs):
            p = jnp.exp(s - m)
            l = l + p.sum(axis=-1, keepdims=True)
            acc = acc + _dot(p.astype(CDT), v)
        o_ref[:, h * HEAD_DIM:(h + 1) * HEAD_DIM] = (acc / l).astype(o_ref.dtype)


def band_call(z1, bias_tiles, o_all, batch, seq):
    tq = BAND_TQ
    assert seq % tq == 0
    nq = seq // tq
    return pl.pallas_call(
        functools.partial(_band_kernel, tq=tq, scale=HEAD_DIM ** -0.5),
        grid=(batch, nq),
        in_specs=[pl.BlockSpec((HB, tq, LANE), lambda b, i: (3, b * nq + i, 0)),
                  pl.BlockSpec((HB, seq, LANE), lambda b, i: (4, b, 0)),
                  pl.BlockSpec((HB, seq, LANE), lambda b, i: (5, b, 0)),
                  pl.BlockSpec(bias_tiles.shape, lambda b, i: (0, 0, 0, 0)),
                  pl.BlockSpec(memory_space=pl.ANY)],
        out_specs=pl.BlockSpec((tq, HB * LANE), lambda b, i: (b * nq + i, 2)),
        out_shape=jax.ShapeDtypeStruct(o_all.shape, o_all.dtype),
        input_output_aliases={4: 0},
        compiler_params=_cparams("parallel", "parallel"),
        name="attn_band",
    )(z1, z1, z1, bias_tiles, o_all)


DSA_TQ = 128
DSA_KSTEP = 512
DSA_HGROUP = 4


def _count(mask):
    return jnp.sum(mask.astype(F32), axis=-1, keepdims=True)


def _dsa_body(qd_ref, kd_ref, vd_ref, qi_ref, ka_ref, kb_ref, wi_ref, o_ref, *, i, tq, ext, k_sel, scale):
    assert IDX_DIM == 64
    w = (wi_ref[...] * (IDX_HEADS ** -0.5)) * (IDX_DIM ** -0.5)
    ka, kb = ka_ref[0, :ext, :], kb_ref[0, :ext, :]
    score = jnp.zeros((tq, ext), F32)
    for p in range(IDX_HEADS // 2):
        qp = qi_ref[p]
        l0 = _dot_nt(qp, ka)
        l1 = _dot_nt(qp, kb)
        c0 = N_HEADS + 2 * p
        score = score + jnp.maximum(l0, 0.0) * w[:, c0:c0 + 1] + jnp.maximum(l1, 0.0) * w[:, c0 + 1:c0 + 2]
    score = score + 0.0
    col = lax.broadcasted_iota(I32, (tq, ext), 1)
    tpos = lax.broadcasted_iota(I32, (tq, 1), 0) + i * tq
    limit = ((tpos >> 6) + 1) << 6
    valid = col < limit
    bits = lax.bitcast_convert_type(score, I32)
    key = jnp.where(bits < 0, bits ^ 0x7FFFFFFF, bits)
    key = jnp.where(valid, key, INT_MIN)

    def thr_body(it, tu):
        cand = tu | lax.shift_left(jnp.int32(1), 31 - it)
        cnt = _count(key >= (cand ^ INT_MIN))
        return jnp.where(cnt >= k_sel, cand, tu)

    tu = lax.fori_loop(0, 32, thr_body, jnp.zeros((tq, 1), I32))
    thr = tu ^ INT_MIN
    gt = key > thr
    ties = (key == thr) & valid
    n_gt = _count(gt)
    need = k_sel - n_gt
    excess = jnp.max(n_gt + _count(ties)) > k_sel

    nbits = max(1, int(np.ceil(np.log2(ext))))

    def tie_search():
        def tie_body(it, j):
            cand = j | lax.shift_left(jnp.int32(1), nbits - 1 - it)
            cnt = _count(ties & (col < cand))
            return jnp.where(cnt < need, cand, j)

        return lax.fori_loop(0, nbits, tie_body, jnp.zeros((tq, 1), I32))

    j = lax.cond(excess, tie_search, lambda: jnp.full((tq, 1), ext, I32))
    sel = gt | (ties & (col <= j))
    maskb = jnp.where(sel, 0.0, NEG)

    kd, vd = kd_ref[0, :ext, :], vd_ref[0, :ext, :]
    for g in range(N_HEADS // DSA_HGROUP):
        q = qd_ref[g * DSA_HGROUP:(g + 1) * DSA_HGROUP].reshape(DSA_HGROUP * tq, HEAD_DIM)
        s = _dot_nt(q, kd) * scale + jnp.tile(maskb, (DSA_HGROUP, 1))
        m = jnp.max(s, axis=-1, keepdims=True)
        p = jnp.exp(s - m)
        l = jnp.sum(p, axis=-1, keepdims=True)
        o = _dot(p.astype(CDT), vd) / l
        for hh in range(DSA_HGROUP):
            h = g * DSA_HGROUP + hh
            o_ref[:, h * HEAD_DIM:(h + 1) * HEAD_DIM] = o[hh * tq:(hh + 1) * tq, :].astype(o_ref.dtype)


def _dsa_kernel(qd_ref, kd_ref, vd_ref, qi_ref, ka_ref, kb_ref, wi_ref, oin_ref, o_ref, *, tq, seq, kstep, k_sel, scale):
    del oin_ref
    i = pl.program_id(1)
    nsteps = ((i + 1) * tq - 1) // kstep + 1
    for n in range(1, seq // kstep + 1):
        @pl.when(nsteps == n)
        def _(n=n):
            _dsa_body(qd_ref, kd_ref, vd_ref, qi_ref, ka_ref, kb_ref, wi_ref, o_ref,
                      i=i, tq=tq, ext=n * kstep, k_sel=k_sel, scale=scale)


def dsa_call(z2, z3, tail, o_all, batch, seq):
    tq = _tile(seq, DSA_TQ)
    nq = seq // tq
    kstep = _tile(seq, DSA_KSTEP)
    k_sel = min(TOPK_MAX, seq // 4)
    return pl.pallas_call(
        functools.partial(_dsa_kernel, tq=tq, seq=seq, kstep=kstep, k_sel=k_sel, scale=HEAD_DIM ** -0.5),
        grid=(batch, nq),
        in_specs=[pl.BlockSpec((HB, tq, LANE), lambda b, i: (0, b * nq + i, 0)),
                  pl.BlockSpec((1, seq, LANE), lambda b, i: (HB, b, 0)),
                  pl.BlockSpec((1, seq, LANE), lambda b, i: (HB + 1, b, 0)),
                  pl.BlockSpec((HB, tq, LANE), lambda b, i: (0, b * nq + i, 0)),
                  pl.BlockSpec((1, seq, LANE), lambda b, i: (HB, b, 0)),
                  pl.BlockSpec((1, seq, LANE), lambda b, i: (HB + 1, b, 0)),
                  pl.BlockSpec((tq, LANE), lambda b, i: (b * nq + i, 0)),
                  pl.BlockSpec(memory_space=pl.ANY)],
        out_specs=pl.BlockSpec((tq, HB * LANE), lambda b, i: (b * nq + i, 3)),
        out_shape=jax.ShapeDtypeStruct(o_all.shape, o_all.dtype),
        input_output_aliases={7: 0},
        compiler_params=_cparams("parallel", "parallel"),
        name="attn_dsa",
    )(z2, z2, z2, z3, z3, z3, tail, o_all)


def _merge_kernel(u_ref, o_ref, wg_ref, bg_ref, wb_ref, out_ref, acc_ref, *, nbranch):
    n = pl.program_id(2)
    g = _dot(u_ref[...], wg_ref[0]) + bg_ref[0]
    term = _sigmoid(g) * _dot(o_ref[...], wb_ref[0])

    @pl.when(n == 0)
    def _():
        acc_ref[...] = term

    @pl.when(n > 0)
    def _():
        acc_ref[...] += term

    @pl.when(n == nbranch - 1)
    def _():
        out_ref[...] = acc_ref[...].astype(out_ref.dtype)


def merge_call(u, o_all, wg, bg, wb):
    t, d = u.shape
    nbranch, bw = wb.shape[0], wb.shape[1]
    tm, tn = _tile(t, 1024), _tile(d, 512)
    return pl.pallas_call(
        functools.partial(_merge_kernel, nbranch=nbranch),
        grid=(t // tm, d // tn, nbranch),
        in_specs=[pl.BlockSpec((tm, d), lambda i, j, n: (i, 0)),
                  pl.BlockSpec((tm, bw), lambda i, j, n: (i, n)),
                  pl.BlockSpec((1, d, tn), lambda i, j, n: (n, 0, j)),
                  pl.BlockSpec((1, 1, tn), lambda i, j, n: (n, 0, j)),
                  pl.BlockSpec((1, bw, tn), lambda i, j, n: (n, 0, j))],
        out_specs=pl.BlockSpec((tm, tn), lambda i, j, n: (i, j)),
        out_shape=jax.ShapeDtypeStruct((t, d), CDT),
        scratch_shapes=[pltpu.VMEM((tm, tn), F32)],
        compiler_params=_cparams("parallel", "parallel", "arbitrary"),
        name="gated_merge",
    )(u, o_all, wg, bg, wb)


def _pack_w_in(w_in):
    off = np.concatenate([[0], np.cumsum(SPLIT_SIZES)])
    col = {n: w_in[:, off[k]:off[k + 1]] for k, n in enumerate(SPLIT_NAMES)}
    d = w_in.shape[0]
    z64 = jnp.zeros((d, 64), w_in.dtype)
    w1 = jnp.concatenate([col[n] for n in ("qa", "ka", "va", "qc", "kc", "vc")], axis=1)
    w2 = jnp.concatenate([col["qd"], col["kd"], col["vd"]], axis=1)
    w3 = jnp.concatenate([col["qi"], col["ki"], z64, z64, col["ki"], col["kr"], z64], axis=1)
    w4 = jnp.concatenate([col["cq"], col["ckv"]], axis=1)
    w5 = jnp.concatenate([col["fa"], col["wi"], jnp.zeros((d, LANE - 24), w_in.dtype)], axis=1)
    w5t = jnp.concatenate([w5[:, :24], jnp.zeros((d, TAIL_ROWS - 24), w_in.dtype)], axis=1).T
    return [w.astype(CDT) for w in (w1, w2, w3, w4, w5, w5t)]


def _pack_w_uq(w_uq):
    r = w_uq.shape[0]
    w = w_uq.reshape(r, N_HEADS, MLA_NOPE + MLA_ROPE)
    w = jnp.concatenate([w, jnp.zeros((r, N_HEADS, 2 * LANE - MLA_NOPE - MLA_ROPE), w_uq.dtype)], axis=-1)
    return w.reshape(r, N_HEADS * 2 * LANE).astype(CDT)


def kernel(x, p, w_in, b_f, g_cq, g_ckv, w_uq, w_ukv, rel_bias, w_branch, w_gate, b_gate, w_out, w1_gate, w1_up, w1_down, w2_gate, w2_up, w2_down, g_ffn1_pre, g_ffn1_post, g_mix_pre, g_mix_post, g_ffn2_pre, g_ffn2_post, g_ple_pre, g_ple_post, w_ple, w_ple_gate):
    batch, seq, d = x.shape
    depth = w_in.shape[0]
    t = batch * seq
    cos128, sin128 = _rope_tables(seq, HEAD_DIM // 2)
    cos64, sin64 = _rope_tables(seq, IDX_DIM // 2)

    h = x.reshape(t, d)
    u = rmsnorm_call(h, g_ffn1_pre[0])
    for i in range(depth):
        a = swiglu_call(u, w1_gate[i].astype(CDT), w1_up[i].astype(CDT))
        h, u = rowres_call(a, w1_down[i].astype(CDT), h, g_ffn1_post[i], g_mix_pre[i], 0.5)

        w1, w2, w3, w4, w5, w5t = _pack_w_in(w_in[i])
        z1 = mm_call(u, w1)
        z2 = proj_rope_call(u, w2, cos128, sin128, seq, HEAD_DIM // 2, range(HB + 1))
        z3 = proj_rope_call(u, w3, cos64, sin64, seq, IDX_DIM // 2, range(HB + 3))
        qm, kvb = proj_mla_call(u, w4, g_cq[i], g_ckv[i], _pack_w_uq(w_uq[i]), w_ukv[i].astype(CDT), cos64, sin64, seq)
        tail, tailt = proj_tail_call(u, w5, w5t, batch, seq)
        drow = decay_call(tailt, b_f[i], batch, seq)
        o_all = fox_call(z1, drow, batch, seq)
        o_all = mla_call(qm, kvb, z3, HB + 2, o_all, batch, seq)
        o_all = band_call(z1, band_bias_call(rel_bias[i]), o_all, batch, seq)
        o_all = dsa_call(z2, z3, tail, o_all, batch, seq)
        merged = merge_call(u, o_all, w_gate[i].astype(CDT), b_gate[i].reshape(-1, 1, d), w_branch[i].astype(CDT))
        h, u = rowres_call(merged, w_out[i].astype(CDT), h, g_mix_post[i], g_ffn2_pre[i], 1.0)

        a = swiglu_call(u, w2_gate[i].astype(CDT), w2_up[i].astype(CDT))
        h, u = rowres_call(a, w2_down[i].astype(CDT), h, g_ffn2_post[i], g_ple_pre[i], 0.5)

        last = i == depth - 1
        g_next = g_ple_pre[i] if last else g_ffn1_pre[i + 1]
        h, u = rowres_call(u, w_ple_gate[i].astype(CDT), h, g_ple_post[i], g_next, 1.0,
                           ple=(p[i].reshape(t, -1), w_ple[i].astype(CDT)), emit_u=not last)
    return h.reshape(batch, seq, d)
```

```python
import functools

import numpy as np
import jax
import jax.numpy as jnp
from jax import lax
from jax.experimental import pallas as pl
from jax.experimental.pallas import tpu as pltpu

F32 = jnp.float32
I32 = jnp.int32
CDT = jnp.bfloat16
EPS = 1e-6
NEG = -1e30
ROPE_THETA = 10000.0

LANE = 128
HEAD_DIM = 128
N_HEADS = 8
HB = N_HEADS * HEAD_DIM // LANE
CHUNK = 64
MLA_Q_LORA, MLA_KV_LORA, MLA_NOPE, MLA_ROPE = 768, 512, 128, 64
CH_PREV, MAX_REL = 8, 128
IDX_HEADS, IDX_DIM = 16, 64
TOPK_MAX = 256
SPLIT_SIZES = (1024, 1024, 1024, 8, 768, 512, 64, 1024, 1024, 1024, 1024, 128, 128, 1024, 64, 16)
SPLIT_NAMES = ("qa", "ka", "va", "fa", "cq", "ckv", "kr", "qc", "kc", "vc", "qd", "kd", "vd", "qi", "ki", "wi")
INT_MIN = -2147483648

VMEM_LIMIT_BYTES = 62 * 1024 * 1024


def _cparams(*sem):
    return pltpu.CompilerParams(dimension_semantics=sem, vmem_limit_bytes=VMEM_LIMIT_BYTES)


def _tile(n, pref):
    t = min(n, pref)
    assert n % t == 0, (n, pref)
    return t


def _dot(a, b):
    return jnp.dot(a, b, preferred_element_type=F32)


def _dot_nt(a, b):
    return lax.dot_general(a, b, (((1,), (1,)), ((), ())), preferred_element_type=F32)


def _rms(y, g):
    ms = jnp.mean(y * y, axis=-1, keepdims=True)
    return y * lax.rsqrt(ms + EPS) * g


def _sigmoid(x):
    return 1.0 / (1.0 + jnp.exp(-x))


def _rmsnorm_kernel(x_ref, g_ref, o_ref):
    o_ref[...] = _rms(x_ref[...], g_ref[...]).astype(o_ref.dtype)


def rmsnorm_call(x, g):
    t, d = x.shape
    tm = _tile(t, 256)
    return pl.pallas_call(
        _rmsnorm_kernel,
        grid=(t // tm,),
        in_specs=[pl.BlockSpec((tm, d), lambda i: (i, 0)), pl.BlockSpec((1, d), lambda i: (0, 0))],
        out_specs=pl.BlockSpec((tm, d), lambda i: (i, 0)),
        out_shape=jax.ShapeDtypeStruct((t, d), CDT),
        compiler_params=_cparams("parallel"),
        name="rmsnorm",
    )(x, g.reshape(1, d))


def _swiglu_kernel(a_ref, wg_ref, wu_ref, o_ref):
    a = a_ref[...]
    g = _dot(a, wg_ref[...])
    v = _dot(a, wu_ref[...])
    o_ref[...] = (g * _sigmoid(g) * v).astype(o_ref.dtype)


def swiglu_call(u, wg, wu):
    t, d = u.shape
    f = wg.shape[1]
    tm, tn = _tile(t, 1024), _tile(f, 512)
    return pl.pallas_call(
        _swiglu_kernel,
        grid=(t // tm, f // tn),
        in_specs=[pl.BlockSpec((tm, d), lambda i, j: (i, 0)),
                  pl.BlockSpec((d, tn), lambda i, j: (0, j)),
                  pl.BlockSpec((d, tn), lambda i, j: (0, j))],
        out_specs=pl.BlockSpec((tm, tn), lambda i, j: (i, j)),
        out_shape=jax.ShapeDtypeStruct((t, f), CDT),
        compiler_params=_cparams("parallel", "parallel"),
        name="swiglu_up",
    )(u, wg, wu)


ROW_CHUNK = 64
ROWRES_TK = 4096


def _rowres_unpack(refs, ple, emit_u):
    a_ref, w_ref, h_ref, gp_ref, gn_ref = refs[:5]
    pos = 5
    p_ref = wple_ref = None
    if ple:
        p_ref, wple_ref = refs[pos:pos + 2]
        pos += 2
    hout_ref = refs[pos]
    uout_ref = refs[pos + 1] if emit_u else None
    return a_ref, w_ref, h_ref, gp_ref, gn_ref, p_ref, wple_ref, hout_ref, uout_ref


def _rowres_epilogue(h_ref, gp_ref, gn_ref, p_ref, wple_ref, hout_ref, uout_ref, coef):
    tm = hout_ref.shape[0]
    rc = min(ROW_CHUNK, tm)

    def body(r, carry):
        rows = pl.ds(pl.multiple_of(r * rc, rc), rc)
        y = hout_ref[rows, :]
        if p_ref is not None:
            e = _dot(p_ref[rows, :].astype(CDT), wple_ref[...])
            y = _sigmoid(y) * e
        hn = h_ref[rows, :] + coef * _rms(y, gp_ref[...])
        hout_ref[rows, :] = hn
        if uout_ref is not None:
            uout_ref[rows, :] = _rms(hn, gn_ref[...]).astype(uout_ref.dtype)
        return carry

    lax.fori_loop(0, tm // rc, body, 0)


def _rowres_kernel(*refs, coef, nk, nj, tn, ple, emit_u):
    a_ref, w_ref, h_ref, gp_ref, gn_ref, p_ref, wple_ref, hout_ref, uout_ref = _rowres_unpack(refs, ple, emit_u)
    k, j = pl.program_id(1), pl.program_id(2)
    cols = pl.ds(pl.multiple_of(j * tn, tn), tn)
    part = _dot(a_ref[...], w_ref[...])
    if nk == 1:
        hout_ref[:, cols] = part
    else:
        @pl.when(k == 0)
        def _():
            hout_ref[:, cols] = part

        @pl.when(k > 0)
        def _():
            hout_ref[:, cols] += part

    @pl.when((k == nk - 1) & (j == nj - 1))
    def _():
        _rowres_epilogue(h_ref, gp_ref, gn_ref, p_ref, wple_ref, hout_ref, uout_ref, coef)


def _rowres_resident_kernel(*refs, coef, ple, emit_u):
    a_ref, w_ref, h_ref, gp_ref, gn_ref, p_ref, wple_ref, hout_ref, uout_ref = _rowres_unpack(refs, ple, emit_u)
    a = a_ref[...]
    d = w_ref.shape[1]
    tn = min(RESIDENT_TN, d)
    for c in range(0, d, tn):
        hout_ref[:, c:c + tn] = _dot(a, w_ref[:, c:c + tn])
    _rowres_epilogue(h_ref, gp_ref, gn_ref, p_ref, wple_ref, hout_ref, uout_ref, coef)


RESIDENT_W_BYTES = 32 * 1024 * 1024
RESIDENT_TM = 256
RESIDENT_TN = 512


def _rowres_resident_call(a, w, h, g_post, g_next, coef, ple, emit_u):
    t, kdim = a.shape
    d = w.shape[1]
    tm = _tile(t, RESIDENT_TM)
    once = pl.Buffered(1)
    in_specs = [pl.BlockSpec((tm, kdim), lambda i: (i, 0)),
                pl.BlockSpec((kdim, d), lambda i: (0, 0), pipeline_mode=once),
                pl.BlockSpec((tm, d), lambda i: (i, 0)),
                pl.BlockSpec((1, d), lambda i: (0, 0), pipeline_mode=once),
                pl.BlockSpec((1, d), lambda i: (0, 0), pipeline_mode=once)]
    args = [a, w, h, g_post.reshape(1, d), g_next.reshape(1, d)]
    if ple is not None:
        p, wple = ple
        in_specs += [pl.BlockSpec((tm, p.shape[1]), lambda i: (i, 0)),
                     pl.BlockSpec(wple.shape, lambda i: (0, 0), pipeline_mode=once)]
        args += [p, wple]
    out_specs = [pl.BlockSpec((tm, d), lambda i: (i, 0))]
    out_shape = [jax.ShapeDtypeStruct((t, d), F32)]
    if emit_u:
        out_specs.append(pl.BlockSpec((tm, d), lambda i: (i, 0)))
        out_shape.append(jax.ShapeDtypeStruct((t, d), CDT))
    res = pl.pallas_call(
        functools.partial(_rowres_resident_kernel, coef=coef, ple=ple is not None, emit_u=emit_u),
        grid=(t // tm,),
        in_specs=in_specs,
        out_specs=out_specs,
        out_shape=out_shape,
        compiler_params=_cparams("arbitrary"),
        name="rowres_ple" if ple is not None else "rowres_res",
    )(*args)
    return (res[0], res[1]) if emit_u else (res[0], None)


def rowres_call(a, w, h, g_post, g_next, coef, ple=None, emit_u=True):
    t, kdim = a.shape
    d = w.shape[1]
    if kdim * d * jnp.dtype(w.dtype).itemsize <= RESIDENT_W_BYTES:
        return _rowres_resident_call(a, w, h, g_post, g_next, coef, ple, emit_u)
    tm, tk, tn = _tile(t, 512), _tile(kdim, ROWRES_TK), _tile(d, 512)
    nk, nj = kdim // tk, d // tn
    once = pl.Buffered(1)
    in_specs = [pl.BlockSpec((tm, tk), lambda i, k, j: (i, k)),
                pl.BlockSpec((tk, tn), lambda i, k, j: (k, j)),
                pl.BlockSpec((tm, d), lambda i, k, j: (i, 0), pipeline_mode=once),
                pl.BlockSpec((1, d), lambda i, k, j: (0, 0), pipeline_mode=once),
                pl.BlockSpec((1, d), lambda i, k, j: (0, 0), pipeline_mode=once)]
    args = [a, w, h, g_post.reshape(1, d), g_next.reshape(1, d)]
    if ple is not None:
        p, wple = ple
        in_specs += [pl.BlockSpec((tm, p.shape[1]), lambda i, k, j: (i, 0), pipeline_mode=once),
                     pl.BlockSpec(wple.shape, lambda i, k, j: (0, 0), pipeline_mode=once)]
        args += [p, wple]
    out_specs = [pl.BlockSpec((tm, d), lambda i, k, j: (i, 0))]
    out_shape = [jax.ShapeDtypeStruct((t, d), F32)]
    if emit_u:
        out_specs.append(pl.BlockSpec((tm, d), lambda i, k, j: (i, 0)))
        out_shape.append(jax.ShapeDtypeStruct((t, d), CDT))
    res = pl.pallas_call(
        functools.partial(_rowres_kernel, coef=coef, nk=nk, nj=nj, tn=tn, ple=ple is not None, emit_u=emit_u),
        grid=(t // tm, nk, nj),
        in_specs=in_specs,
        out_specs=out_specs,
        out_shape=out_shape,
        compiler_params=_cparams("parallel", "arbitrary", "arbitrary"),
        name="rowres_ple" if ple is not None else "rowres",
    )(*args)
    return (res[0], res[1]) if emit_u else (res[0], None)


def _mm_kernel(a_ref, w_ref, o_ref):
    acc = _dot(a_ref[...], w_ref[...])
    for blk in range(o_ref.shape[0]):
        o_ref[blk] = acc[:, blk * LANE:(blk + 1) * LANE].astype(o_ref.dtype)


def mm_call(a, w):
    t, d = a.shape
    n = w.shape[1]
    tm, tn = _tile(t, 1024), _tile(n, 1024)
    return pl.pallas_call(
        _mm_kernel,
        grid=(t // tm, n // tn),
        in_specs=[pl.BlockSpec((tm, d), lambda i, j: (i, 0)), pl.BlockSpec((d, tn), lambda i, j: (0, j))],
        out_specs=pl.BlockSpec((tn // LANE, tm, LANE), lambda i, j: (j, i, 0)),
        out_shape=jax.ShapeDtypeStruct((n // LANE, t, LANE), CDT),
        compiler_params=_cparams("parallel", "parallel"),
        name="proj_plain",
    )(a, w)


def _rope_tables(seq, half):
    inv = ROPE_THETA ** (-jnp.arange(half, dtype=F32) / half)
    ang = jnp.arange(seq, dtype=F32)[:, None] * inv[None, :]
    cos, sin = jnp.cos(ang), jnp.sin(ang)
    reps = LANE // (2 * half)
    return jnp.tile(jnp.concatenate([cos, cos], -1), (1, reps)), jnp.tile(jnp.concatenate([-sin, sin], -1), (1, reps))


def _rope_block(x, c, s, half):
    if 2 * half == LANE:
        sw = pltpu.roll(x, half, axis=1)
    else:
        lane = lax.broadcasted_iota(I32, x.shape, 1)
        first = (lane & (2 * half - 1)) < half
        sw = jnp.where(first, pltpu.roll(x, LANE - half, axis=1), pltpu.roll(x, half, axis=1))
    return x * c + sw * s


def _proj_rope_kernel(a_ref, w_ref, c_ref, s_ref, o_ref, *, half, rope_blocks):
    acc = _dot(a_ref[...], w_ref[...])
    c, s = c_ref[...], s_ref[...]
    for blk in range(o_ref.shape[0]):
        x = acc[:, blk * LANE:(blk + 1) * LANE]
        if blk in rope_blocks:
            x = _rope_block(x, c, s, half)
        o_ref[blk] = x.astype(o_ref.dtype)


def proj_rope_call(u, w, cos, sin, seq, half, rope_blocks):
    t, d = u.shape
    n = w.shape[1]
    tm = _tile(seq, 512)
    ns = seq // tm
    return pl.pallas_call(
        functools.partial(_proj_rope_kernel, half=half, rope_blocks=tuple(rope_blocks)),
        grid=(t // tm,),
        in_specs=[pl.BlockSpec((tm, d), lambda i: (i, 0)),
                  pl.BlockSpec((d, n), lambda i: (0, 0)),
                  pl.BlockSpec((tm, LANE), lambda i: (i % ns, 0)),
                  pl.BlockSpec((tm, LANE), lambda i: (i % ns, 0))],
        out_specs=pl.BlockSpec((n // LANE, tm, LANE), lambda i: (0, i, 0)),
        out_shape=jax.ShapeDtypeStruct((n // LANE, t, LANE), CDT),
        compiler_params=_cparams("parallel"),
        name=f"proj_rope{2 * half}",
    )(u, w, cos, sin)


def _proj_mla_kernel(a_ref, w_ref, gq_ref, gkv_ref, wuq_ref, wukv_ref, c_ref, s_ref, q_ref, kv_ref):
    acc = _dot(a_ref[...], w_ref[...])
    cq = _rms(acc[:, :MLA_Q_LORA], gq_ref[...]).astype(CDT)
    ckv = _rms(acc[:, MLA_Q_LORA:MLA_Q_LORA + MLA_KV_LORA], gkv_ref[...]).astype(CDT)
    qb = _dot(cq, wuq_ref[...])
    c, s = c_ref[...], s_ref[...]
    for blk in range(q_ref.shape[0]):
        x = qb[:, blk * LANE:(blk + 1) * LANE]
        if blk % 2 == 1:
            x = _rope_block(x, c, s, MLA_ROPE // 2)
        q_ref[blk] = x.astype(q_ref.dtype)
    kvb = _dot(ckv, wukv_ref[...])
    for blk in range(kv_ref.shape[0]):
        kv_ref[blk] = kvb[:, blk * LANE:(blk + 1) * LANE].astype(kv_ref.dtype)


def proj_mla_call(u, w, gq, gkv, wuq, wukv, cos, sin, seq):
    t, d = u.shape
    n = w.shape[1]
    tm = _tile(seq, 512)
    ns = seq // tm
    nq, nkv = wuq.shape[1] // LANE, wukv.shape[1] // LANE
    return pl.pallas_call(
        _proj_mla_kernel,
        grid=(t // tm,),
        in_specs=[pl.BlockSpec((tm, d), lambda i: (i, 0)),
                  pl.BlockSpec((d, n), lambda i: (0, 0)),
                  pl.BlockSpec((1, MLA_Q_LORA), lambda i: (0, 0)),
                  pl.BlockSpec((1, MLA_KV_LORA), lambda i: (0, 0)),
                  pl.BlockSpec(wuq.shape, lambda i: (0, 0)),
                  pl.BlockSpec(wukv.shape, lambda i: (0, 0)),
                  pl.BlockSpec((tm, LANE), lambda i: (i % ns, 0)),
                  pl.BlockSpec((tm, LANE), lambda i: (i % ns, 0))],
        out_specs=[pl.BlockSpec((nq, tm, LANE), lambda i: (0, i, 0)), pl.BlockSpec((nkv, tm, LANE), lambda i: (0, i, 0))],
        out_shape=[jax.ShapeDtypeStruct((nq, t, LANE), CDT), jax.ShapeDtypeStruct((nkv, t, LANE), CDT)],
        compiler_params=_cparams("parallel"),
        name="proj_mla",
    )(u, w, gq.reshape(1, -1), gkv.reshape(1, -1), wuq, wukv, cos, sin)


TAIL_ROWS = 32


def _proj_tail_kernel(a_ref, w_ref, wt_ref, o_ref, ot_ref):
    a = a_ref[...]
    o_ref[...] = _dot(a, w_ref[...])
    ot_ref[0] = _dot_nt(wt_ref[...], a)


def proj_tail_call(u, w, wt, batch, seq):
    t, d = u.shape
    tm = _tile(seq, 512)
    ns = seq // tm
    return pl.pallas_call(
        _proj_tail_kernel,
        grid=(t // tm,),
        in_specs=[pl.BlockSpec((tm, d), lambda i: (i, 0)),
                  pl.BlockSpec((d, LANE), lambda i: (0, 0)),
                  pl.BlockSpec((TAIL_ROWS, d), lambda i: (0, 0))],
        out_specs=[pl.BlockSpec((tm, LANE), lambda i: (i, 0)),
                   pl.BlockSpec((1, TAIL_ROWS, tm), lambda i: (i // ns, 0, i % ns))],
        out_shape=[jax.ShapeDtypeStruct((t, LANE), F32), jax.ShapeDtypeStruct((batch, TAIL_ROWS, seq), F32)],
        compiler_params=_cparams("parallel"),
        name="proj_tail",
    )(u, w, wt)


def _log_sigmoid(x):
    return jnp.minimum(x, 0.0) - jnp.log1p(jnp.exp(-jnp.abs(x)))


def _decay_kernel(tailt_ref, brow_ref, drow_ref):
    y = _log_sigmoid(tailt_ref[0, :N_HEADS, :] + brow_ref[...])
    seq = y.shape[1]
    idy = lax.broadcasted_iota(I32, y.shape, 1)
    d = 1
    while d < seq:
        y = y + jnp.where(idy >= d, pltpu.roll(y, d, axis=1), 0.0)
        d *= 2
    drow_ref[0] = y


def decay_call(tailt, b_f, batch, seq):
    return pl.pallas_call(
        _decay_kernel,
        grid=(batch,),
        in_specs=[pl.BlockSpec((1, TAIL_ROWS, seq), lambda b: (b, 0, 0)),
                  pl.BlockSpec((N_HEADS, 1), lambda b: (0, 0))],
        out_specs=pl.BlockSpec((1, N_HEADS, seq), lambda b: (b, 0, 0)),
        out_shape=jax.ShapeDtypeStruct((batch, N_HEADS, seq), F32),
        compiler_params=_cparams("parallel"),
        name="decay_scan",
    )(tailt, b_f.reshape(N_HEADS, 1))


ATT_TQ = 512
ATT_SUB = 2


def _prefix_attention(i, tq, nq, logits_fn, v_fn, allowed_fn, o_ref):
    sq = tq // ATT_SUB
    row = lax.broadcasted_iota(I32, (sq, sq), 0)
    col = lax.broadcasted_iota(I32, (sq, sq), 1)
    for n in range(nq):
        @pl.when(i == n)
        def _(n=n):
            for r in range(ATT_SUB):
                pre = n * tq + r * sq
                sd = jnp.where(allowed_fn(row, col), logits_fn(r, pre, sq), NEG)
                m = jnp.max(sd, axis=-1, keepdims=True)
                if pre:
                    sp = logits_fn(r, 0, pre)
                    m = jnp.maximum(m, jnp.max(sp, axis=-1, keepdims=True))
                pd = jnp.exp(sd - m)
                l = jnp.sum(pd, axis=-1, keepdims=True)
                acc = _dot(pd.astype(CDT), v_fn(pre, sq))
                if pre:
                    pp = jnp.exp(sp - m)
                    l = l + jnp.sum(pp, axis=-1, keepdims=True)
                    acc = acc + _dot(pp.astype(CDT), v_fn(0, pre))
                o_ref[r * sq:(r + 1) * sq, :] = (acc / l).astype(o_ref.dtype)


def _fox_kernel(q_ref, k_ref, v_ref, drow_ref, o_ref, *, tq, nq, scale):
    sq = tq // ATT_SUB
    qs = [q_ref[0, r * sq:(r + 1) * sq, :] for r in range(ATT_SUB)]

    def logits(r, start, n):
        return _dot_nt(qs[r], k_ref[0, start:start + n, :]) * scale - drow_ref[0, :, start:start + n]

    _prefix_attention(pl.program_id(2), tq, nq, logits, lambda start, n: v_ref[0, start:start + n, :],
                      lambda row, col: col <= row, o_ref)


def _mla_kernel(qn_ref, qr_ref, kn_ref, kr_ref, v_ref, oin_ref, o_ref, *, tq, nq, scale):
    del oin_ref
    sq = tq // ATT_SUB
    qs = [jnp.concatenate([qn_ref[0, r * sq:(r + 1) * sq, :], qr_ref[0, r * sq:(r + 1) * sq, :]], axis=1)
          for r in range(ATT_SUB)]

    def logits(r, start, n):
        k = jnp.concatenate([kn_ref[0, start:start + n, :], kr_ref[0, start:start + n, :]], axis=1)
        return _dot_nt(qs[r], k) * scale

    _prefix_attention(pl.program_id(2), tq, nq, logits, lambda start, n: v_ref[0, start:start + n, :],
                      lambda row, col: (col >> 6) <= (row >> 6), o_ref)


def fox_call(z1, drow, batch, seq):
    tq = _tile(seq, ATT_TQ)
    nq = seq // tq
    drow = drow.reshape(batch * N_HEADS, 1, seq)
    t = batch * seq
    return pl.pallas_call(
        functools.partial(_fox_kernel, tq=tq, nq=nq, scale=HEAD_DIM ** -0.5),
        grid=(batch, N_HEADS, nq),
        in_specs=[pl.BlockSpec((1, tq, LANE), lambda b, h, i: (h, b * nq + i, 0)),
                  pl.BlockSpec((1, seq, LANE), lambda b, h, i: (HB + h, b, 0)),
                  pl.BlockSpec((1, seq, LANE), lambda b, h, i: (2 * HB + h, b, 0)),
                  pl.BlockSpec((1, 1, seq), lambda b, h, i: (b * N_HEADS + h, 0, 0))],
        out_specs=pl.BlockSpec((tq, LANE), lambda b, h, i: (b * nq + i, h)),
        out_shape=jax.ShapeDtypeStruct((t, 4 * HB * LANE), CDT),
        compiler_params=_cparams("parallel", "parallel", "parallel"),
        name="attn_fox",
    )(z1, z1, z1, drow)


def mla_call(qm, kvb, z3, kr_block, o_all, batch, seq):
    tq = _tile(seq, ATT_TQ)
    nq = seq // tq
    return pl.pallas_call(
        functools.partial(_mla_kernel, tq=tq, nq=nq, scale=(MLA_NOPE + MLA_ROPE) ** -0.5),
        grid=(batch, N_HEADS, nq),
        in_specs=[pl.BlockSpec((1, tq, LANE), lambda b, h, i: (2 * h, b * nq + i, 0)),
                  pl.BlockSpec((1, tq, LANE), lambda b, h, i: (2 * h + 1, b * nq + i, 0)),
                  pl.BlockSpec((1, seq, LANE), lambda b, h, i: (2 * h, b, 0)),
                  pl.BlockSpec((1, seq, LANE), lambda b, h, i: (kr_block, b, 0)),
                  pl.BlockSpec((1, seq, LANE), lambda b, h, i: (2 * h + 1, b, 0)),
                  pl.BlockSpec(memory_space=pl.ANY)],
        out_specs=pl.BlockSpec((tq, LANE), lambda b, h, i: (b * nq + i, HB + h)),
        out_shape=jax.ShapeDtypeStruct(o_all.shape, o_all.dtype),
        input_output_aliases={5: 0},
        compiler_params=_cparams("parallel", "parallel", "parallel"),
        name="attn_mla",
    )(qm, qm, kvb, z3, kvb, o_all)


BAND_TQ = 256
BAND_NKB = (CH_PREV * CHUNK) // BAND_TQ + 1
BIAS_STRIP = 128


def _band_bias_kernel(tab_ref, o_ref):
    h = pl.program_id(0)
    back = (BAND_NKB - 1) * BAND_TQ
    for kb in range(BAND_NKB):
        for q0 in range(0, BAND_TQ, BIAS_STRIP):
            for k0 in range(0, BAND_TQ, BIAS_STRIP):
                q = lax.broadcasted_iota(I32, (BIAS_STRIP, BIAS_STRIP), 0) + q0
                koff = lax.broadcasted_iota(I32, (BIAS_STRIP, BIAS_STRIP), 1) + (kb * BAND_TQ + k0 - back)
                rel = jnp.clip(q - koff, -MAX_REL, MAX_REL) + MAX_REL
                koff0 = kb * BAND_TQ + k0 - back
                lo = int(np.clip(q0 - (koff0 + BIAS_STRIP - 1), -MAX_REL, MAX_REL)) + MAX_REL
                hi = int(np.clip(q0 + BIAS_STRIP - 1 - koff0, -MAX_REL, MAX_REL)) + MAX_REL

                def body(r, acc, rel=rel):
                    return jnp.where(rel == r, tab_ref[h, r], acc)

                acc = lax.fori_loop(lo, hi + 1, body, jnp.zeros((BIAS_STRIP, BIAS_STRIP), F32))
                kc, qc = koff >> 6, q >> 6
                band = (kc >= qc - CH_PREV) & (kc <= qc)
                o_ref[0, kb, q0:q0 + BIAS_STRIP, k0:k0 + BIAS_STRIP] = jnp.where(band, acc, NEG)


def band_bias_call(rel_table):
    return pl.pallas_call(
        _band_bias_kernel,
        grid=(N_HEADS,),
        in_specs=[pl.BlockSpec(memory_space=pltpu.SMEM)],
        out_specs=pl.BlockSpec((1, BAND_NKB, BAND_TQ, BAND_TQ), lambda h: (h, 0, 0, 0)),
        out_shape=jax.ShapeDtypeStruct((N_HEADS, BAND_NKB, BAND_TQ, BAND_TQ), F32),
        compiler_params=_cparams("parallel"),
        name="band_bias",
    )(rel_table.astype(F32))


def _band_kernel(q_ref, k_ref, v_ref, bias_ref, oin_ref, o_ref, *, tq, scale):
    del oin_ref
    i = pl.program_id(1)
    for h in range(N_HEADS):
        q = q_ref[h]
        ss, vs = [], []
        for kb in range(BAND_NKB):
            kt = i - (BAND_NKB - 1) + kb
            ks = pl.ds(pl.multiple_of(jnp.maximum(kt, 0) * tq, tq), tq)
            s = _dot_nt(q, k_ref[h, ks, :]) * scale + bias_ref[h, kb]
            if kb < BAND_NKB - 1:
                s = jnp.where(kt >= 0, s, NEG)
            ss.append(s)
            vs.append(v_ref[h, ks, :])
        m = ss[0].max(axis=-1, keepdims=True)
        for s in ss[1:]:
            m = jnp.maximum(m, s.max(axis=-1, keepdims=True))
        l = jnp.zeros((tq, 1), F32)
        acc = jnp.zeros((tq, HEAD_DIM), F32)
        for s, v in zip(ss, vs):
            p = jnp.exp(s - m)
            l = l + p.sum(axis=-1, keepdims=True)
            acc = acc + _dot(p.astype(CDT), v)
        o_ref[:, h * HEAD_DIM:(h + 1) * HEAD_DIM] = (acc / l).astype(o_ref.dtype)


def band_call(z1, bias_tiles, o_all, batch, seq):
    tq = BAND_TQ
    assert seq % tq == 0
    nq = seq // tq
    return pl.pallas_call(
        functools.partial(_band_kernel, tq=tq, scale=HEAD_DIM ** -0.5),
        grid=(batch, nq),
        in_specs=[pl.BlockSpec((HB, tq, LANE), lambda b, i: (3, b * nq + i, 0)),
                  pl.BlockSpec((HB, seq, LANE), lambda b, i: (4, b, 0)),
                  pl.BlockSpec((HB, seq, LANE), lambda b, i: (5, b, 0)),
                  pl.BlockSpec(bias_tiles.shape, lambda b, i: (0, 0, 0, 0)),
                  pl.BlockSpec(memory_space=pl.ANY)],
        out_specs=pl.BlockSpec((tq, HB * LANE), lambda b, i: (b * nq + i, 2)),
        out_shape=jax.ShapeDtypeStruct(o_all.shape, o_all.dtype),
        input_output_aliases={4: 0},
        compiler_params=_cparams("parallel", "parallel"),
        name="attn_band",
    )(z1, z1, z1, bias_tiles, o_all)


DSA_TQ = 256
DSA_KSTEP = 512
DSA_HGROUP = 2


def _count(mask):
    return jnp.sum(mask.astype(F32), axis=-1, keepdims=True)


def _dsa_body(qd_ref, kd_ref, vd_ref, qi_ref, ka_ref, kb_ref, wi_ref, o_ref, *, i, tq, ext, k_sel, scale):
    assert IDX_DIM == 64
    w = (wi_ref[...] * (IDX_HEADS ** -0.5)) * (IDX_DIM ** -0.5)
    ka, kb = ka_ref[0, :ext, :], kb_ref[0, :ext, :]
    score = jnp.zeros((tq, ext), F32)
    for p in range(IDX_HEADS // 2):
        qp = qi_ref[p]
        l0 = _dot_nt(qp, ka)
        l1 = _dot_nt(qp, kb)
        c0 = N_HEADS + 2 * p
        score = score + jnp.maximum(l0, 0.0) * w[:, c0:c0 + 1] + jnp.maximum(l1, 0.0) * w[:, c0 + 1:c0 + 2]
    score = score + 0.0
    col = lax.broadcasted_iota(I32, (tq, ext), 1)
    tpos = lax.broadcasted_iota(I32, (tq, 1), 0) + i * tq
    limit = ((tpos >> 6) + 1) << 6
    valid = col < limit
    bits = lax.bitcast_convert_type(score, I32)
    key = jnp.where(bits < 0, bits ^ 0x7FFFFFFF, bits)
    key = jnp.where(valid, key, INT_MIN)

    def thr_body(it, tu):
        cand = tu | lax.shift_left(jnp.int32(1), 31 - it)
        cnt = _count(key >= (cand ^ INT_MIN))
        return jnp.where(cnt >= k_sel, cand, tu)

    tu = lax.fori_loop(0, 32, thr_body, jnp.zeros((tq, 1), I32))
    thr = tu ^ INT_MIN
    gt = key > thr
    ties = (key == thr) & valid
    n_gt = _count(gt)
    need = k_sel - n_gt
    excess = jnp.max(n_gt + _count(ties)) > k_sel

    nbits = max(1, int(np.ceil(np.log2(ext))))

    def tie_search():
        def tie_body(it, j):
            cand = j | lax.shift_left(jnp.int32(1), nbits - 1 - it)
            cnt = _count(ties & (col < cand))
            return jnp.where(cnt < need, cand, j)

        return lax.fori_loop(0, nbits, tie_body, jnp.zeros((tq, 1), I32))

    j = lax.cond(excess, tie_search, lambda: jnp.full((tq, 1), ext, I32))
    sel = gt | (ties & (col <= j))
    maskb = jnp.where(sel, 0.0, NEG)

    kd, vd = kd_ref[0, :ext, :], vd_ref[0, :ext, :]
    for g in range(N_HEADS // DSA_HGROUP):
        q = qd_ref[g * DSA_HGROUP:(g + 1) * DSA_HGROUP].reshape(DSA_HGROUP * tq, HEAD_DIM)
        s = _dot_nt(q, kd) * scale + jnp.tile(maskb, (DSA_HGROUP, 1))
        m = jnp.max(s, axis=-1, keepdims=True)
        p = jnp.exp(s - m)
        l = jnp.sum(p, axis=-1, keepdims=True)
        o = _dot(p.astype(CDT), vd) / l
        for hh in range(DSA_HGROUP):
            h = g * DSA_HGROUP + hh
            o_ref[:, h * HEAD_DIM:(h + 1) * HEAD_DIM] = o[hh * tq:(hh + 1) * tq, :].astype(o_ref.dtype)


def _dsa_kernel(qd_ref, kd_ref, vd_ref, qi_ref, ka_ref, kb_ref, wi_ref, oin_ref, o_ref, *, tq, seq, kstep, k_sel, scale):
    del oin_ref
    i = pl.program_id(1)
    nsteps = ((i + 1) * tq - 1) // kstep + 1
    for n in range(1, seq // kstep + 1):
        @pl.when(nsteps == n)
        def _(n=n):
            _dsa_body(qd_ref, kd_ref, vd_ref, qi_ref, ka_ref, kb_ref, wi_ref, o_ref,
                      i=i, tq=tq, ext=n * kstep, k_sel=k_sel, scale=scale)


def dsa_call(z2, z3, tail, o_all, batch, seq):
    tq = _tile(seq, DSA_TQ)
    nq = seq // tq
    kstep = _tile(seq, DSA_KSTEP)
    k_sel = min(TOPK_MAX, seq // 4)
    return pl.pallas_call(
        functools.partial(_dsa_kernel, tq=tq, seq=seq, kstep=kstep, k_sel=k_sel, scale=HEAD_DIM ** -0.5),
        grid=(batch, nq),
        in_specs=[pl.BlockSpec((HB, tq, LANE), lambda b, i: (0, b * nq + i, 0)),
                  pl.BlockSpec((1, seq, LANE), lambda b, i: (HB, b, 0)),
                  pl.BlockSpec((1, seq, LANE), lambda b, i: (HB + 1, b, 0)),
                  pl.BlockSpec((HB, tq, LANE), lambda b, i: (0, b * nq + i, 0)),
                  pl.BlockSpec((1, seq, LANE), lambda b, i: (HB, b, 0)),
                  pl.BlockSpec((1, seq, LANE), lambda b, i: (HB + 1, b, 0)),
                  pl.BlockSpec((tq, LANE), lambda b, i: (b * nq + i, 0)),
                  pl.BlockSpec(memory_space=pl.ANY)],
        out_specs=pl.BlockSpec((tq, HB * LANE), lambda b, i: (b * nq + i, 3)),
        out_shape=jax.ShapeDtypeStruct(o_all.shape, o_all.dtype),
        input_output_aliases={7: 0},
        compiler_params=_cparams("parallel", "parallel"),
        name="attn_dsa",
    )(z2, z2, z2, z3, z3, z3, tail, o_all)


def _merge_kernel(u_ref, o_ref, wg_ref, bg_ref, wb_ref, out_ref, acc_ref, *, nbranch):
    n = pl.program_id(2)
    g = _dot(u_ref[...], wg_ref[0]) + bg_ref[0]
    term = _sigmoid(g) * _dot(o_ref[...], wb_ref[0])

    @pl.when(n == 0)
    def _():
        acc_ref[...] = term

    @pl.when(n > 0)
    def _():
        acc_ref[...] += term

    @pl.when(n == nbranch - 1)
    def _():
        out_ref[...] = acc_ref[...].astype(out_ref.dtype)


def merge_call(u, o_all, wg, bg, wb):
    t, d = u.shape
    nbranch, bw = wb.shape[0], wb.shape[1]
    tm, tn = _tile(t, 1024), _tile(d, 1024)
    return pl.pallas_call(
        functools.partial(_merge_kernel, nbranch=nbranch),
        grid=(t // tm, d // tn, nbranch),
        in_specs=[pl.BlockSpec((tm, d), lambda i, j, n: (i, 0)),
                  pl.BlockSpec((tm, bw), lambda i, j, n: (i, n)),
                  pl.BlockSpec((1, d, tn), lambda i, j, n: (n, 0, j)),
                  pl.BlockSpec((1, 1, tn), lambda i, j, n: (n, 0, j)),
                  pl.BlockSpec((1, bw, tn), lambda i, j, n: (n, 0, j))],
        out_specs=pl.BlockSpec((tm, tn), lambda i, j, n: (i, j)),
        out_shape=jax.ShapeDtypeStruct((t, d), CDT),
        scratch_shapes=[pltpu.VMEM((tm, tn), F32)],
        compiler_params=_cparams("parallel", "parallel", "arbitrary"),
        name="gated_merge",
    )(u, o_all, wg, bg, wb)


def _pack_w_in(w_in):
    off = np.concatenate([[0], np.cumsum(SPLIT_SIZES)])
    col = {n: w_in[:, off[k]:off[k + 1]] for k, n in enumerate(SPLIT_NAMES)}
    d = w_in.shape[0]
    z64 = jnp.zeros((d, 64), w_in.dtype)
    w1 = jnp.concatenate([col[n] for n in ("qa", "ka", "va", "qc", "kc", "vc")], axis=1)
    w2 = jnp.concatenate([col["qd"], col["kd"], col["vd"]], axis=1)
    w3 = jnp.concatenate([col["qi"], col["ki"], z64, z64, col["ki"], col["kr"], z64], axis=1)
    w4 = jnp.concatenate([col["cq"], col["ckv"]], axis=1)
    w5 = jnp.concatenate([col["fa"], col["wi"], jnp.zeros((d, LANE - 24), w_in.dtype)], axis=1)
    w5t = jnp.concatenate([w5[:, :24], jnp.zeros((d, TAIL_ROWS - 24), w_in.dtype)], axis=1).T
    return [w.astype(CDT) for w in (w1, w2, w3, w4, w5, w5t)]


def _pack_w_uq(w_uq):
    r = w_uq.shape[0]
    w = w_uq.reshape(r, N_HEADS, MLA_NOPE + MLA_ROPE)
    w = jnp.concatenate([w, jnp.zeros((r, N_HEADS, 2 * LANE - MLA_NOPE - MLA_ROPE), w_uq.dtype)], axis=-1)
    return w.reshape(r, N_HEADS * 2 * LANE).astype(CDT)


def kernel(x, p, w_in, b_f, g_cq, g_ckv, w_uq, w_ukv, rel_bias, w_branch, w_gate, b_gate, w_out, w1_gate, w1_up, w1_down, w2_gate, w2_up, w2_down, g_ffn1_pre, g_ffn1_post, g_mix_pre, g_mix_post, g_ffn2_pre, g_ffn2_post, g_ple_pre, g_ple_post, w_ple, w_ple_gate):
    batch, seq, d = x.shape
    depth = w_in.shape[0]
    t = batch * seq
    cos128, sin128 = _rope_tables(seq, HEAD_DIM // 2)
    cos64, sin64 = _rope_tables(seq, IDX_DIM // 2)

    h = x.reshape(t, d)
    u = rmsnorm_call(h, g_ffn1_pre[0])
    for i in range(depth):
        a = swiglu_call(u, w1_gate[i].astype(CDT), w1_up[i].astype(CDT))
        h, u = rowres_call(a, w1_down[i].astype(CDT), h, g_ffn1_post[i], g_mix_pre[i], 0.5)

        w1, w2, w3, w4, w5, w5t = _pack_w_in(w_in[i])
        z1 = mm_call(u, w1)
        z2 = proj_rope_call(u, w2, cos128, sin128, seq, HEAD_DIM // 2, range(HB + 1))
        z3 = proj_rope_call(u, w3, cos64, sin64, seq, IDX_DIM // 2, range(HB + 3))
        qm, kvb = proj_mla_call(u, w4, g_cq[i], g_ckv[i], _pack_w_uq(w_uq[i]), w_ukv[i].astype(CDT), cos64, sin64, seq)
        tail, tailt = proj_tail_call(u, w5, w5t, batch, seq)
        drow = decay_call(tailt, b_f[i], batch, seq)
        o_all = fox_call(z1, drow, batch, seq)
        o_all = mla_call(qm, kvb, z3, HB + 2, o_all, batch, seq)
        o_all = band_call(z1, band_bias_call(rel_bias[i]), o_all, batch, seq)
        o_all = dsa_call(z2, z3, tail, o_all, batch, seq)
        merged = merge_call(u, o_all, w_gate[i].astype(CDT), b_gate[i].reshape(-1, 1, d), w_branch[i].astype(CDT))
        h, u = rowres_call(merged, w_out[i].astype(CDT), h, g_mix_post[i], g_ffn2_pre[i], 1.0)

        a = swiglu_call(u, w2_gate[i].astype(CDT), w2_up[i].astype(CDT))
        h, u = rowres_call(a, w2_down[i].astype(CDT), h, g_ffn2_post[i], g_ple_pre[i], 0.5)

        last = i == depth - 1
        g_next = g_ple_pre[i] if last else g_ffn1_pre[i + 1]
        h, u = rowres_call(u, w_ple_gate[i].astype(CDT), h, g_ple_post[i], g_next, 1.0,
                           ple=(p[i].reshape(t, -1), w_ple[i].astype(CDT)), emit_u=not last)
    return h.reshape(batch, seq, d)
```

```python
import functools

import numpy as np
import jax
import jax.numpy as jnp
from jax import lax
from jax.experimental import pallas as pl
from jax.experimental.pallas import tpu as pltpu

F32 = jnp.float32
I32 = jnp.int32
CDT = jnp.bfloat16
EPS = 1e-6
NEG = -1e30
ROPE_THETA = 10000.0

LANE = 128
HEAD_DIM = 128
N_HEADS = 8
HB = N_HEADS * HEAD_DIM // LANE
CHUNK = 64
MLA_Q_LORA, MLA_KV_LORA, MLA_NOPE, MLA_ROPE = 768, 512, 128, 64
CH_PREV, MAX_REL = 8, 128
IDX_HEADS, IDX_DIM = 16, 64
TOPK_MAX = 256
SPLIT_SIZES = (1024, 1024, 1024, 8, 768, 512, 64, 1024, 1024, 1024, 1024, 128, 128, 1024, 64, 16)
SPLIT_NAMES = ("qa", "ka", "va", "fa", "cq", "ckv", "kr", "qc", "kc", "vc", "qd", "kd", "vd", "qi", "ki", "wi")
INT_MIN = -2147483648

VMEM_LIMIT_BYTES = 62 * 1024 * 1024


def _cparams(*sem):
    return pltpu.CompilerParams(dimension_semantics=sem, vmem_limit_bytes=VMEM_LIMIT_BYTES)


def _tile(n, pref):
    t = min(n, pref)
    assert n % t == 0, (n, pref)
    return t


def _dot(a, b):
    return jnp.dot(a, b, preferred_element_type=F32)


def _dot_nt(a, b):
    return lax.dot_general(a, b, (((1,), (1,)), ((), ())), preferred_element_type=F32)


def _rms(y, g):
    ms = jnp.mean(y * y, axis=-1, keepdims=True)
    return y * lax.rsqrt(ms + EPS) * g


def _sigmoid(x):
    return 1.0 / (1.0 + jnp.exp(-x))


def _rmsnorm_kernel(x_ref, g_ref, o_ref):
    o_ref[...] = _rms(x_ref[...], g_ref[...]).astype(o_ref.dtype)


def rmsnorm_call(x, g):
    t, d = x.shape
    tm = _tile(t, 256)
    return pl.pallas_call(
        _rmsnorm_kernel,
        grid=(t // tm,),
        in_specs=[pl.BlockSpec((tm, d), lambda i: (i, 0)), pl.BlockSpec((1, d), lambda i: (0, 0))],
        out_specs=pl.BlockSpec((tm, d), lambda i: (i, 0)),
        out_shape=jax.ShapeDtypeStruct((t, d), CDT),
        compiler_params=_cparams("parallel"),
        name="rmsnorm",
    )(x, g.reshape(1, d))


def _swiglu_kernel(a_ref, wg_ref, wu_ref, o_ref):
    a = a_ref[...]
    g = _dot(a, wg_ref[...])
    v = _dot(a, wu_ref[...])
    o_ref[...] = (g * _sigmoid(g) * v).astype(o_ref.dtype)


def swiglu_call(u, wg, wu, layer):
    t, d = u.shape
    f = wg.shape[2]
    tm, tn = _tile(t, 1024), _tile(f, 512)
    return pl.pallas_call(
        _swiglu_kernel,
        grid=(t // tm, f // tn),
        in_specs=[pl.BlockSpec((tm, d), lambda i, j: (i, 0)),
                  pl.BlockSpec((None, d, tn), lambda i, j: (layer, 0, j)),
                  pl.BlockSpec((None, d, tn), lambda i, j: (layer, 0, j))],
        out_specs=pl.BlockSpec((tm, tn), lambda i, j: (i, j)),
        out_shape=jax.ShapeDtypeStruct((t, f), CDT),
        compiler_params=_cparams("parallel", "parallel"),
        name="swiglu_up",
    )(u, wg, wu)


ROW_CHUNK = 64
ROWRES_TK = 4096
ROWRES_TN = 1024


def _rowres_unpack(refs, ple, emit_u):
    a_ref, w_ref, h_ref, gp_ref, gn_ref = refs[:5]
    pos = 5
    p_ref = wple_ref = None
    if ple:
        p_ref, wple_ref = refs[pos:pos + 2]
        pos += 2
    hout_ref = refs[pos]
    uout_ref = refs[pos + 1] if emit_u else None
    return a_ref, w_ref, h_ref, gp_ref, gn_ref, p_ref, wple_ref, hout_ref, uout_ref


def _rowres_epilogue(h_ref, gp_ref, gn_ref, p_ref, wple_ref, hout_ref, uout_ref, coef):
    tm = hout_ref.shape[0]
    rc = min(ROW_CHUNK, tm)

    def body(r, carry):
        rows = pl.ds(pl.multiple_of(r * rc, rc), rc)
        y = hout_ref[rows, :]
        if p_ref is not None:
            e = _dot(p_ref[rows, :].astype(CDT), wple_ref[...])
            y = _sigmoid(y) * e
        hn = h_ref[rows, :] + coef * _rms(y, gp_ref[...])
        hout_ref[rows, :] = hn
        if uout_ref is not None:
            uout_ref[rows, :] = _rms(hn, gn_ref[...]).astype(uout_ref.dtype)
        return carry

    lax.fori_loop(0, tm // rc, body, 0)


def _rowres_kernel(*refs, coef, nk, nj, tn, ple, emit_u):
    a_ref, w_ref, h_ref, gp_ref, gn_ref, p_ref, wple_ref, hout_ref, uout_ref = _rowres_unpack(refs, ple, emit_u)
    k, j = pl.program_id(1), pl.program_id(2)
    cols = pl.ds(pl.multiple_of(j * tn, tn), tn)
    part = _dot(a_ref[...], w_ref[...])
    if nk == 1:
        hout_ref[:, cols] = part
    else:
        @pl.when(k == 0)
        def _():
            hout_ref[:, cols] = part

        @pl.when(k > 0)
        def _():
            hout_ref[:, cols] += part

    @pl.when((k == nk - 1) & (j == nj - 1))
    def _():
        _rowres_epilogue(h_ref, gp_ref, gn_ref, p_ref, wple_ref, hout_ref, uout_ref, coef)


def _rowres_resident_kernel(*refs, coef, ple, emit_u):
    a_ref, w_ref, h_ref, gp_ref, gn_ref, p_ref, wple_ref, hout_ref, uout_ref = _rowres_unpack(refs, ple, emit_u)
    a = a_ref[...]
    d = w_ref.shape[1]
    tn = min(RESIDENT_TN, d)
    for c in range(0, d, tn):
        hout_ref[:, c:c + tn] = _dot(a, w_ref[:, c:c + tn])
    _rowres_epilogue(h_ref, gp_ref, gn_ref, p_ref, wple_ref, hout_ref, uout_ref, coef)


RESIDENT_W_BYTES = 32 * 1024 * 1024
RESIDENT_TM = 256
RESIDENT_TN = 512


def _rowres_resident_call(a, w, layer, h, g_post, g_next, coef, ple, emit_u):
    t, kdim = a.shape
    d = w.shape[2]
    tm = _tile(t, RESIDENT_TM)
    once = pl.Buffered(1)
    in_specs = [pl.BlockSpec((tm, kdim), lambda i: (i, 0)),
                pl.BlockSpec((None, kdim, d), lambda i: (layer, 0, 0), pipeline_mode=once),
                pl.BlockSpec((tm, d), lambda i: (i, 0)),
                pl.BlockSpec((1, d), lambda i: (0, 0), pipeline_mode=once),
                pl.BlockSpec((1, d), lambda i: (0, 0), pipeline_mode=once)]
    args = [a, w, h, g_post.reshape(1, d), g_next.reshape(1, d)]
    if ple is not None:
        p, wple = ple
        in_specs += [pl.BlockSpec((tm, p.shape[1]), lambda i: (i, 0)),
                     pl.BlockSpec(wple.shape, lambda i: (0, 0), pipeline_mode=once)]
        args += [p, wple]
    out_specs = [pl.BlockSpec((tm, d), lambda i: (i, 0))]
    out_shape = [jax.ShapeDtypeStruct((t, d), F32)]
    if emit_u:
        out_specs.append(pl.BlockSpec((tm, d), lambda i: (i, 0)))
        out_shape.append(jax.ShapeDtypeStruct((t, d), CDT))
    res = pl.pallas_call(
        functools.partial(_rowres_resident_kernel, coef=coef, ple=ple is not None, emit_u=emit_u),
        grid=(t // tm,),
        in_specs=in_specs,
        out_specs=out_specs,
        out_shape=out_shape,
        compiler_params=_cparams("arbitrary"),
        name="rowres_ple" if ple is not None else "rowres_res",
    )(*args)
    return (res[0], res[1]) if emit_u else (res[0], None)


def rowres_call(a, w, layer, h, g_post, g_next, coef, ple=None, emit_u=True):
    t, kdim = a.shape
    d = w.shape[2]
    if kdim * d * jnp.dtype(w.dtype).itemsize <= RESIDENT_W_BYTES:
        return _rowres_resident_call(a, w, layer, h, g_post, g_next, coef, ple, emit_u)
    tm, tk, tn = _tile(t, 512), _tile(kdim, ROWRES_TK), _tile(d, ROWRES_TN)
    nk, nj = kdim // tk, d // tn
    once = pl.Buffered(1)
    in_specs = [pl.BlockSpec((tm, tk), lambda i, k, j: (i, k)),
                pl.BlockSpec((None, tk, tn), lambda i, k, j: (layer, k, j)),
                pl.BlockSpec((tm, d), lambda i, k, j: (i, 0), pipeline_mode=once),
                pl.BlockSpec((1, d), lambda i, k, j: (0, 0), pipeline_mode=once),
                pl.BlockSpec((1, d), lambda i, k, j: (0, 0), pipeline_mode=once)]
    args = [a, w, h, g_post.reshape(1, d), g_next.reshape(1, d)]
    if ple is not None:
        p, wple = ple
        in_specs += [pl.BlockSpec((tm, p.shape[1]), lambda i, k, j: (i, 0), pipeline_mode=once),
                     pl.BlockSpec(wple.shape, lambda i, k, j: (0, 0), pipeline_mode=once)]
        args += [p, wple]
    out_specs = [pl.BlockSpec((tm, d), lambda i, k, j: (i, 0))]
    out_shape = [jax.ShapeDtypeStruct((t, d), F32)]
    if emit_u:
        out_specs.append(pl.BlockSpec((tm, d), lambda i, k, j: (i, 0)))
        out_shape.append(jax.ShapeDtypeStruct((t, d), CDT))
    res = pl.pallas_call(
        functools.partial(_rowres_kernel, coef=coef, nk=nk, nj=nj, tn=tn, ple=ple is not None, emit_u=emit_u),
        grid=(t // tm, nk, nj),
        in_specs=in_specs,
        out_specs=out_specs,
        out_shape=out_shape,
        compiler_params=_cparams("parallel", "arbitrary", "arbitrary"),
        name="rowres_ple" if ple is not None else "rowres",
    )(*args)
    return (res[0], res[1]) if emit_u else (res[0], None)


def _mm_kernel(a_ref, w_ref, o_ref):
    acc = _dot(a_ref[...], w_ref[...])
    for blk in range(o_ref.shape[0]):
        o_ref[blk] = acc[:, blk * LANE:(blk + 1) * LANE].astype(o_ref.dtype)


def mm_call(a, w):
    t, d = a.shape
    n = w.shape[1]
    tm, tn = _tile(t, 1024), _tile(n, 1024)
    return pl.pallas_call(
        _mm_kernel,
        grid=(t // tm, n // tn),
        in_specs=[pl.BlockSpec((tm, d), lambda i, j: (i, 0)), pl.BlockSpec((d, tn), lambda i, j: (0, j))],
        out_specs=pl.BlockSpec((tn // LANE, tm, LANE), lambda i, j: (j, i, 0)),
        out_shape=jax.ShapeDtypeStruct((n // LANE, t, LANE), CDT),
        compiler_params=_cparams("parallel", "parallel"),
        name="proj_plain",
    )(a, w)


def _rope_tables(seq, half):
    inv = ROPE_THETA ** (-jnp.arange(half, dtype=F32) / half)
    ang = jnp.arange(seq, dtype=F32)[:, None] * inv[None, :]
    cos, sin = jnp.cos(ang), jnp.sin(ang)
    reps = LANE // (2 * half)
    return jnp.tile(jnp.concatenate([cos, cos], -1), (1, reps)), jnp.tile(jnp.concatenate([-sin, sin], -1), (1, reps))


def _rope_block(x, c, s, half):
    if 2 * half == LANE:
        sw = pltpu.roll(x, half, axis=1)
    else:
        lane = lax.broadcasted_iota(I32, x.shape, 1)
        first = (lane & (2 * half - 1)) < half
        sw = jnp.where(first, pltpu.roll(x, LANE - half, axis=1), pltpu.roll(x, half, axis=1))
    return x * c + sw * s


def _proj_rope_kernel(a_ref, w_ref, c_ref, s_ref, o_ref, *, half, rope_blocks):
    acc = _dot(a_ref[...], w_ref[...])
    c, s = c_ref[...], s_ref[...]
    for blk in range(o_ref.shape[0]):
        x = acc[:, blk * LANE:(blk + 1) * LANE]
        if blk in rope_blocks:
            x = _rope_block(x, c, s, half)
        o_ref[blk] = x.astype(o_ref.dtype)


def proj_rope_call(u, w, cos, sin, seq, half, rope_blocks):
    t, d = u.shape
    n = w.shape[1]
    tm = _tile(seq, 512)
    ns = seq // tm
    return pl.pallas_call(
        functools.partial(_proj_rope_kernel, half=half, rope_blocks=tuple(rope_blocks)),
        grid=(t // tm,),
        in_specs=[pl.BlockSpec((tm, d), lambda i: (i, 0)),
                  pl.BlockSpec((d, n), lambda i: (0, 0)),
                  pl.BlockSpec((tm, LANE), lambda i: (i % ns, 0)),
                  pl.BlockSpec((tm, LANE), lambda i: (i % ns, 0))],
        out_specs=pl.BlockSpec((n // LANE, tm, LANE), lambda i: (0, i, 0)),
        out_shape=jax.ShapeDtypeStruct((n // LANE, t, LANE), CDT),
        compiler_params=_cparams("parallel"),
        name=f"proj_rope{2 * half}",
    )(u, w, cos, sin)


def _proj_mla_kernel(a_ref, w_ref, gq_ref, gkv_ref, wuq_ref, wukv_ref, c_ref, s_ref, q_ref, kv_ref):
    acc = _dot(a_ref[...], w_ref[...])
    cq = _rms(acc[:, :MLA_Q_LORA], gq_ref[...]).astype(CDT)
    ckv = _rms(acc[:, MLA_Q_LORA:MLA_Q_LORA + MLA_KV_LORA], gkv_ref[...]).astype(CDT)
    qb = _dot(cq, wuq_ref[...])
    c, s = c_ref[...], s_ref[...]
    for blk in range(q_ref.shape[0]):
        x = qb[:, blk * LANE:(blk + 1) * LANE]
        if blk % 2 == 1:
            x = _rope_block(x, c, s, MLA_ROPE // 2)
        q_ref[blk] = x.astype(q_ref.dtype)
    kvb = _dot(ckv, wukv_ref[...])
    for blk in range(kv_ref.shape[0]):
        kv_ref[blk] = kvb[:, blk * LANE:(blk + 1) * LANE].astype(kv_ref.dtype)


def proj_mla_call(u, w, gq, gkv, wuq, wukv, cos, sin, seq):
    t, d = u.shape
    n = w.shape[1]
    tm = _tile(seq, 512)
    ns = seq // tm
    nq, nkv = wuq.shape[1] // LANE, wukv.shape[1] // LANE
    return pl.pallas_call(
        _proj_mla_kernel,
        grid=(t // tm,),
        in_specs=[pl.BlockSpec((tm, d), lambda i: (i, 0)),
                  pl.BlockSpec((d, n), lambda i: (0, 0)),
                  pl.BlockSpec((1, MLA_Q_LORA), lambda i: (0, 0)),
                  pl.BlockSpec((1, MLA_KV_LORA), lambda i: (0, 0)),
                  pl.BlockSpec(wuq.shape, lambda i: (0, 0)),
                  pl.BlockSpec(wukv.shape, lambda i: (0, 0)),
                  pl.BlockSpec((tm, LANE), lambda i: (i % ns, 0)),
                  pl.BlockSpec((tm, LANE), lambda i: (i % ns, 0))],
        out_specs=[pl.BlockSpec((nq, tm, LANE), lambda i: (0, i, 0)), pl.BlockSpec((nkv, tm, LANE), lambda i: (0, i, 0))],
        out_shape=[jax.ShapeDtypeStruct((nq, t, LANE), CDT), jax.ShapeDtypeStruct((nkv, t, LANE), CDT)],
        compiler_params=_cparams("parallel"),
        name="proj_mla",
    )(u, w, gq.reshape(1, -1), gkv.reshape(1, -1), wuq, wukv, cos, sin)


TAIL_ROWS = 32


def _proj_tail_kernel(a_ref, w_ref, wt_ref, o_ref, ot_ref):
    a = a_ref[...]
    o_ref[...] = _dot(a, w_ref[...])
    ot_ref[0] = _dot_nt(wt_ref[...], a)


def proj_tail_call(u, w, wt, batch, seq):
    t, d = u.shape
    tm = _tile(seq, 512)
    ns = seq // tm
    return pl.pallas_call(
        _proj_tail_kernel,
        grid=(t // tm,),
        in_specs=[pl.BlockSpec((tm, d), lambda i: (i, 0)),
                  pl.BlockSpec((d, LANE), lambda i: (0, 0)),
                  pl.BlockSpec((TAIL_ROWS, d), lambda i: (0, 0))],
        out_specs=[pl.BlockSpec((tm, LANE), lambda i: (i, 0)),
                   pl.BlockSpec((1, TAIL_ROWS, tm), lambda i: (i // ns, 0, i % ns))],
        out_shape=[jax.ShapeDtypeStruct((t, LANE), F32), jax.ShapeDtypeStruct((batch, TAIL_ROWS, seq), F32)],
        compiler_params=_cparams("parallel"),
        name="proj_tail",
    )(u, w, wt)


def _log_sigmoid(x):
    return jnp.minimum(x, 0.0) - jnp.log1p(jnp.exp(-jnp.abs(x)))


def _decay_kernel(tailt_ref, brow_ref, drow_ref):
    y = _log_sigmoid(tailt_ref[0, :N_HEADS, :] + brow_ref[...])
    seq = y.shape[1]
    idy = lax.broadcasted_iota(I32, y.shape, 1)
    d = 1
    while d < seq:
        y = y + jnp.where(idy >= d, pltpu.roll(y, d, axis=1), 0.0)
        d *= 2
    drow_ref[0] = y


def decay_call(tailt, b_f, batch, seq):
    return pl.pallas_call(
        _decay_kernel,
        grid=(batch,),
        in_specs=[pl.BlockSpec((1, TAIL_ROWS, seq), lambda b: (b, 0, 0)),
                  pl.BlockSpec((N_HEADS, 1), lambda b: (0, 0))],
        out_specs=pl.BlockSpec((1, N_HEADS, seq), lambda b: (b, 0, 0)),
        out_shape=jax.ShapeDtypeStruct((batch, N_HEADS, seq), F32),
        compiler_params=_cparams("parallel"),
        name="decay_scan",
    )(tailt, b_f.reshape(N_HEADS, 1))


ATT_TQ = 512
ATT_SUB = 2


def _prefix_attention(i, tq, nq, logits_fn, v_fn, allowed_fn, o_ref):
    sq = tq // ATT_SUB
    row = lax.broadcasted_iota(I32, (sq, sq), 0)
    col = lax.broadcasted_iota(I32, (sq, sq), 1)
    for n in range(nq):
        @pl.when(i == n)
        def _(n=n):
            for r in range(ATT_SUB):
                pre = n * tq + r * sq
                sd = jnp.where(allowed_fn(row, col), logits_fn(r, pre, sq), NEG)
                m = jnp.max(sd, axis=-1, keepdims=True)
                if pre:
                    sp = logits_fn(r, 0, pre)
                    m = jnp.maximum(m, jnp.max(sp, axis=-1, keepdims=True))
                pd = jnp.exp(sd - m)
                l = jnp.sum(pd, axis=-1, keepdims=True)
                acc = _dot(pd.astype(CDT), v_fn(pre, sq))
                if pre:
                    pp = jnp.exp(sp - m)
                    l = l + jnp.sum(pp, axis=-1, keepdims=True)
                    acc = acc + _dot(pp.astype(CDT), v_fn(0, pre))
                o_ref[r * sq:(r + 1) * sq, :] = (acc / l).astype(o_ref.dtype)


def _fox_kernel(q_ref, k_ref, v_ref, drow_ref, o_ref, *, tq, nq, scale):
    sq = tq // ATT_SUB
    qs = [q_ref[0, r * sq:(r + 1) * sq, :] for r in range(ATT_SUB)]

    def logits(r, start, n):
        return _dot_nt(qs[r], k_ref[0, start:start + n, :]) * scale - drow_ref[0, :, start:start + n]

    _prefix_attention(pl.program_id(2), tq, nq, logits, lambda start, n: v_ref[0, start:start + n, :],
                      lambda row, col: col <= row, o_ref)


def _mla_kernel(qn_ref, qr_ref, kn_ref, kr_ref, v_ref, oin_ref, o_ref, *, tq, nq, scale):
    del oin_ref
    sq = tq // ATT_SUB
    qs = [jnp.concatenate([qn_ref[0, r * sq:(r + 1) * sq, :], qr_ref[0, r * sq:(r + 1) * sq, :]], axis=1)
          for r in range(ATT_SUB)]

    def logits(r, start, n):
        k = jnp.concatenate([kn_ref[0, start:start + n, :], kr_ref[0, start:start + n, :]], axis=1)
        return _dot_nt(qs[r], k) * scale

    _prefix_attention(pl.program_id(2), tq, nq, logits, lambda start, n: v_ref[0, start:start + n, :],
                      lambda row, col: (col >> 6) <= (row >> 6), o_ref)


def fox_call(z1, drow, batch, seq):
    tq = _tile(seq, ATT_TQ)
    nq = seq // tq
    drow = drow.reshape(batch * N_HEADS, 1, seq)
    t = batch * seq
    return pl.pallas_call(
        functools.partial(_fox_kernel, tq=tq, nq=nq, scale=HEAD_DIM ** -0.5),
        grid=(batch, N_HEADS, nq),
        in_specs=[pl.BlockSpec((1, tq, LANE), lambda b, h, i: (h, b * nq + i, 0)),
                  pl.BlockSpec((1, seq, LANE), lambda b, h, i: (HB + h, b, 0)),
                  pl.BlockSpec((1, seq, LANE), lambda b, h, i: (2 * HB + h, b, 0)),
                  pl.BlockSpec((1, 1, seq), lambda b, h, i: (b * N_HEADS + h, 0, 0))],
        out_specs=pl.BlockSpec((tq, LANE), lambda b, h, i: (b * nq + i, h)),
        out_shape=jax.ShapeDtypeStruct((t, 4 * HB * LANE), CDT),
        compiler_params=_cparams("parallel", "parallel", "parallel"),
        name="attn_fox",
    )(z1, z1, z1, drow)


def mla_call(qm, kvb, z3, kr_block, o_all, batch, seq):
    tq = _tile(seq, ATT_TQ)
    nq = seq // tq
    return pl.pallas_call(
        functools.partial(_mla_kernel, tq=tq, nq=nq, scale=(MLA_NOPE + MLA_ROPE) ** -0.5),
        grid=(batch, N_HEADS, nq),
        in_specs=[pl.BlockSpec((1, tq, LANE), lambda b, h, i: (2 * h, b * nq + i, 0)),
                  pl.BlockSpec((1, tq, LANE), lambda b, h, i: (2 * h + 1, b * nq + i, 0)),
                  pl.BlockSpec((1, seq, LANE), lambda b, h, i: (2 * h, b, 0)),
                  pl.BlockSpec((1, seq, LANE), lambda b, h, i: (kr_block, b, 0)),
                  pl.BlockSpec((1, seq, LANE), lambda b, h, i: (2 * h + 1, b, 0)),
                  pl.BlockSpec(memory_space=pl.ANY)],
        out_specs=pl.BlockSpec((tq, LANE), lambda b, h, i: (b * nq + i, HB + h)),
        out_shape=jax.ShapeDtypeStruct(o_all.shape, o_all.dtype),
        input_output_aliases={5: 0},
        compiler_params=_cparams("parallel", "parallel", "parallel"),
        name="attn_mla",
    )(qm, qm, kvb, z3, kvb, o_all)


BAND_TQ = 256
BAND_NKB = (CH_PREV * CHUNK) // BAND_TQ + 1
BIAS_STRIP = 128


def _band_bias_kernel(tab_ref, o_ref):
    h = pl.program_id(0)
    back = (BAND_NKB - 1) * BAND_TQ
    for kb in range(BAND_NKB):
        for q0 in range(0, BAND_TQ, BIAS_STRIP):
            for k0 in range(0, BAND_TQ, BIAS_STRIP):
                q = lax.broadcasted_iota(I32, (BIAS_STRIP, BIAS_STRIP), 0) + q0
                koff = lax.broadcasted_iota(I32, (BIAS_STRIP, BIAS_STRIP), 1) + (kb * BAND_TQ + k0 - back)
                rel = jnp.clip(q - koff, -MAX_REL, MAX_REL) + MAX_REL
                koff0 = kb * BAND_TQ + k0 - back
                lo = int(np.clip(q0 - (koff0 + BIAS_STRIP - 1), -MAX_REL, MAX_REL)) + MAX_REL
                hi = int(np.clip(q0 + BIAS_STRIP - 1 - koff0, -MAX_REL, MAX_REL)) + MAX_REL

                def body(r, acc, rel=rel):
                    return jnp.where(rel == r, tab_ref[h, r], acc)

                acc = lax.fori_loop(lo, hi + 1, body, jnp.zeros((BIAS_STRIP, BIAS_STRIP), F32))
                kc, qc = koff >> 6, q >> 6
                band = (kc >= qc - CH_PREV) & (kc <= qc)
                o_ref[0, kb, q0:q0 + BIAS_STRIP, k0:k0 + BIAS_STRIP] = jnp.where(band, acc, NEG)


def band_bias_call(rel_table):
    return pl.pallas_call(
        _band_bias_kernel,
        grid=(N_HEADS,),
        in_specs=[pl.BlockSpec(memory_space=pltpu.SMEM)],
        out_specs=pl.BlockSpec((1, BAND_NKB, BAND_TQ, BAND_TQ), lambda h: (h, 0, 0, 0)),
        out_shape=jax.ShapeDtypeStruct((N_HEADS, BAND_NKB, BAND_TQ, BAND_TQ), F32),
        compiler_params=_cparams("parallel"),
        name="band_bias",
    )(rel_table.astype(F32))


def _band_kernel(q_ref, k_ref, v_ref, bias_ref, oin_ref, o_ref, *, tq, scale):
    del oin_ref
    i = pl.program_id(1)
    for h in range(N_HEADS):
        q = q_ref[h]
        ss, vs = [], []
        for kb in range(BAND_NKB):
            kt = i - (BAND_NKB - 1) + kb
            ks = pl.ds(pl.multiple_of(jnp.maximum(kt, 0) * tq, tq), tq)
            s = _dot_nt(q, k_ref[h, ks, :]) * scale + bias_ref[h, kb]
            if kb < BAND_NKB - 1:
                s = jnp.where(kt >= 0, s, NEG)
            ss.append(s)
            vs.append(v_ref[h, ks, :])
        m = ss[0].max(axis=-1, keepdims=True)
        for s in ss[1:]:
            m = jnp.maximum(m, s.max(axis=-1, keepdims=True))
        l = jnp.zeros((tq, 1), F32)
        acc = jnp.zeros((tq, HEAD_DIM), F32)
        for s, v in zip(ss, vs):
            p = jnp.exp(s - m)
            l = l + p.sum(axis=-1, keepdims=True)
            acc = acc + _dot(p.astype(CDT), v)
        o_ref[:, h * HEAD_DIM:(h + 1) * HEAD_DIM] = (acc / l).astype(o_ref.dtype)


def band_call(z1, bias_tiles, o_all, batch, seq):
    tq = BAND_TQ
    assert seq % tq == 0
    nq = seq // tq
    return pl.pallas_call(
        functools.partial(_band_kernel, tq=tq, scale=HEAD_DIM ** -0.5),
        grid=(batch, nq),
        in_specs=[pl.BlockSpec((HB, tq, LANE), lambda b, i: (3, b * nq + i, 0)),
                  pl.BlockSpec((HB, seq, LANE), lambda b, i: (4, b, 0)),
                  pl.BlockSpec((HB, seq, LANE), lambda b, i: (5, b, 0)),
                  pl.BlockSpec(bias_tiles.shape, lambda b, i: (0, 0, 0, 0)),
                  pl.BlockSpec(memory_space=pl.ANY)],
        out_specs=pl.BlockSpec((tq, HB * LANE), lambda b, i: (b * nq + i, 2)),
        out_shape=jax.ShapeDtypeStruct(o_all.shape, o_all.dtype),
        input_output_aliases={4: 0},
        compiler_params=_cparams("parallel", "parallel"),
        name="attn_band",
    )(z1, z1, z1, bias_tiles, o_all)


DSA_TQ = 256
DSA_KSTEP = 512
DSA_HGROUP = 2


def _count(mask):
    return jnp.sum(mask.astype(F32), axis=-1, keepdims=True)


def _dsa_body(qd_ref, kd_ref, vd_ref, qi_ref, ka_ref, kb_ref, wi_ref, o_ref, *, i, tq, ext, k_sel, scale):
    assert IDX_DIM == 64
    w = (wi_ref[...] * (IDX_HEADS ** -0.5)) * (IDX_DIM ** -0.5)
    ka, kb = ka_ref[0, :ext, :], kb_ref[0, :ext, :]
    score = jnp.zeros((tq, ext), F32)
    for p in range(IDX_HEADS // 2):
        qp = qi_ref[p]
        l0 = _dot_nt(qp, ka)
        l1 = _dot_nt(qp, kb)
        c0 = N_HEADS + 2 * p
        score = score + jnp.maximum(l0, 0.0) * w[:, c0:c0 + 1] + jnp.maximum(l1, 0.0) * w[:, c0 + 1:c0 + 2]
    score = score + 0.0
    col = lax.broadcasted_iota(I32, (tq, ext), 1)
    tpos = lax.broadcasted_iota(I32, (tq, 1), 0) + i * tq
    limit = ((tpos >> 6) + 1) << 6
    valid = col < limit
    bits = lax.bitcast_convert_type(score, I32)
    key = jnp.where(bits < 0, bits ^ 0x7FFFFFFF, bits)
    key = jnp.where(valid, key, INT_MIN)

    def thr_body(it, tu):
        cand = tu | lax.shift_left(jnp.int32(1), 31 - it)
        cnt = _count(key >= (cand ^ INT_MIN))
        return jnp.where(cnt >= k_sel, cand, tu)

    tu = lax.fori_loop(0, 32, thr_body, jnp.zeros((tq, 1), I32))
    thr = tu ^ INT_MIN
    gt = key > thr
    ties = (key == thr) & valid
    n_gt = _count(gt)
    need = k_sel - n_gt
    excess = jnp.max(n_gt + _count(ties)) > k_sel

    nbits = max(1, int(np.ceil(np.log2(ext))))

    def tie_search():
        def tie_body(it, j):
            cand = j | lax.shift_left(jnp.int32(1), nbits - 1 - it)
            cnt = _count(ties & (col < cand))
            return jnp.where(cnt < need, cand, j)

        return lax.fori_loop(0, nbits, tie_body, jnp.zeros((tq, 1), I32))

    j = lax.cond(excess, tie_search, lambda: jnp.full((tq, 1), ext, I32))
    sel = gt | (ties & (col <= j))
    maskb = jnp.where(sel, 0.0, NEG)

    kd, vd = kd_ref[0, :ext, :], vd_ref[0, :ext, :]
    for g in range(N_HEADS // DSA_HGROUP):
        q = qd_ref[g * DSA_HGROUP:(g + 1) * DSA_HGROUP].reshape(DSA_HGROUP * tq, HEAD_DIM)
        s = _dot_nt(q, kd) * scale + jnp.tile(maskb, (DSA_HGROUP, 1))
        m = jnp.max(s, axis=-1, keepdims=True)
        p = jnp.exp(s - m)
        l = jnp.sum(p, axis=-1, keepdims=True)
        o = _dot(p.astype(CDT), vd) / l
        for hh in range(DSA_HGROUP):
            h = g * DSA_HGROUP + hh
            o_ref[:, h * HEAD_DIM:(h + 1) * HEAD_DIM] = o[hh * tq:(hh + 1) * tq, :].astype(o_ref.dtype)


def _dsa_kernel(qd_ref, kd_ref, vd_ref, qi_ref, ka_ref, kb_ref, wi_ref, oin_ref, o_ref, *, tq, seq, kstep, k_sel, scale):
    del oin_ref
    i = pl.program_id(1)
    nsteps = ((i + 1) * tq - 1) // kstep + 1
    for n in range(1, seq // kstep + 1):
        @pl.when(nsteps == n)
        def _(n=n):
            _dsa_body(qd_ref, kd_ref, vd_ref, qi_ref, ka_ref, kb_ref, wi_ref, o_ref,
                      i=i, tq=tq, ext=n * kstep, k_sel=k_sel, scale=scale)


def dsa_call(z2, z3, tail, o_all, batch, seq):
    tq = _tile(seq, DSA_TQ)
    nq = seq // tq
    kstep = _tile(seq, DSA_KSTEP)
    k_sel = min(TOPK_MAX, seq // 4)
    return pl.pallas_call(
        functools.partial(_dsa_kernel, tq=tq, seq=seq, kstep=kstep, k_sel=k_sel, scale=HEAD_DIM ** -0.5),
        grid=(batch, nq),
        in_specs=[pl.BlockSpec((HB, tq, LANE), lambda b, i: (0, b * nq + i, 0)),
                  pl.BlockSpec((1, seq, LANE), lambda b, i: (HB, b, 0)),
                  pl.BlockSpec((1, seq, LANE), lambda b, i: (HB + 1, b, 0)),
                  pl.BlockSpec((HB, tq, LANE), lambda b, i: (0, b * nq + i, 0)),
                  pl.BlockSpec((1, seq, LANE), lambda b, i: (HB, b, 0)),
                  pl.BlockSpec((1, seq, LANE), lambda b, i: (HB + 1, b, 0)),
                  pl.BlockSpec((tq, LANE), lambda b, i: (b * nq + i, 0)),
                  pl.BlockSpec(memory_space=pl.ANY)],
        out_specs=pl.BlockSpec((tq, HB * LANE), lambda b, i: (b * nq + i, 3)),
        out_shape=jax.ShapeDtypeStruct(o_all.shape, o_all.dtype),
        input_output_aliases={7: 0},
        compiler_params=_cparams("parallel", "parallel"),
        name="attn_dsa",
    )(z2, z2, z2, z3, z3, z3, tail, o_all)


def _merge_kernel(u_ref, o_ref, wg_ref, bg_ref, wb_ref, out_ref, acc_ref, *, nbranch):
    n = pl.program_id(2)
    g = _dot(u_ref[...], wg_ref[0]) + bg_ref[0]
    term = _sigmoid(g) * _dot(o_ref[...], wb_ref[0])

    @pl.when(n == 0)
    def _():
        acc_ref[...] = term

    @pl.when(n > 0)
    def _():
        acc_ref[...] += term

    @pl.when(n == nbranch - 1)
    def _():
        out_ref[...] = acc_ref[...].astype(out_ref.dtype)


def merge_call(u, o_all, wg, bg, wb, layer):
    t, d = u.shape
    nbranch, bw = wb.shape[1], wb.shape[2]
    tm, tn = _tile(t, 1024), _tile(d, 1024)
    return pl.pallas_call(
        functools.partial(_merge_kernel, nbranch=nbranch),
        grid=(t // tm, d // tn, nbranch),
        in_specs=[pl.BlockSpec((tm, d), lambda i, j, n: (i, 0)),
                  pl.BlockSpec((tm, bw), lambda i, j, n: (i, n)),
                  pl.BlockSpec((None, 1, d, tn), lambda i, j, n: (layer, n, 0, j)),
                  pl.BlockSpec((1, 1, tn), lambda i, j, n: (n, 0, j)),
                  pl.BlockSpec((None, 1, bw, tn), lambda i, j, n: (layer, n, 0, j))],
        out_specs=pl.BlockSpec((tm, tn), lambda i, j, n: (i, j)),
        out_shape=jax.ShapeDtypeStruct((t, d), CDT),
        scratch_shapes=[pltpu.VMEM((tm, tn), F32)],
        compiler_params=_cparams("parallel", "parallel", "arbitrary"),
        name="gated_merge",
    )(u, o_all, wg, bg, wb)


def _pack_w_in(w_in):
    off = np.concatenate([[0], np.cumsum(SPLIT_SIZES)])
    col = {n: w_in[:, off[k]:off[k + 1]] for k, n in enumerate(SPLIT_NAMES)}
    d = w_in.shape[0]
    z64 = jnp.zeros((d, 64), w_in.dtype)
    w1 = jnp.concatenate([col[n] for n in ("qa", "ka", "va", "qc", "kc", "vc")], axis=1)
    w2 = jnp.concatenate([col["qd"], col["kd"], col["vd"]], axis=1)
    w3 = jnp.concatenate([col["qi"], col["ki"], z64, z64, col["ki"], col["kr"], z64], axis=1)
    w4 = jnp.concatenate([col["cq"], col["ckv"]], axis=1)
    w5 = jnp.concatenate([col["fa"], col["wi"], jnp.zeros((d, LANE - 24), w_in.dtype)], axis=1)
    w5t = jnp.concatenate([w5[:, :24], jnp.zeros((d, TAIL_ROWS - 24), w_in.dtype)], axis=1).T
    return [w.astype(CDT) for w in (w1, w2, w3, w4, w5, w5t)]


def _pack_w_uq(w_uq):
    r = w_uq.shape[0]
    w = w_uq.reshape(r, N_HEADS, MLA_NOPE + MLA_ROPE)
    w = jnp.concatenate([w, jnp.zeros((r, N_HEADS, 2 * LANE - MLA_NOPE - MLA_ROPE), w_uq.dtype)], axis=-1)
    return w.reshape(r, N_HEADS * 2 * LANE).astype(CDT)


def kernel(x, p, w_in, b_f, g_cq, g_ckv, w_uq, w_ukv, rel_bias, w_branch, w_gate, b_gate, w_out, w1_gate, w1_up, w1_down, w2_gate, w2_up, w2_down, g_ffn1_pre, g_ffn1_post, g_mix_pre, g_mix_post, g_ffn2_pre, g_ffn2_post, g_ple_pre, g_ple_post, w_ple, w_ple_gate):
    batch, seq, d = x.shape
    depth = w_in.shape[0]
    t = batch * seq
    cos128, sin128 = _rope_tables(seq, HEAD_DIM // 2)
    cos64, sin64 = _rope_tables(seq, IDX_DIM // 2)

    w1_gate, w1_up, w1_down, w2_gate, w2_up, w2_down, w_gate, w_branch, w_out, w_ple_gate = (
        w.astype(CDT) for w in (w1_gate, w1_up, w1_down, w2_gate, w2_up, w2_down, w_gate, w_branch, w_out, w_ple_gate))

    h = x.reshape(t, d)
    u = rmsnorm_call(h, g_ffn1_pre[0])
    for i in range(depth):
        a = swiglu_call(u, w1_gate, w1_up, i)
        h, u = rowres_call(a, w1_down, i, h, g_ffn1_post[i], g_mix_pre[i], 0.5)

        w1, w2, w3, w4, w5, w5t = _pack_w_in(w_in[i])
        z1 = mm_call(u, w1)
        z2 = proj_rope_call(u, w2, cos128, sin128, seq, HEAD_DIM // 2, range(HB + 1))
        z3 = proj_rope_call(u, w3, cos64, sin64, seq, IDX_DIM // 2, range(HB + 3))
        qm, kvb = proj_mla_call(u, w4, g_cq[i], g_ckv[i], _pack_w_uq(w_uq[i]), w_ukv[i].astype(CDT), cos64, sin64, seq)
        tail, tailt = proj_tail_call(u, w5, w5t, batch, seq)
        drow = decay_call(tailt, b_f[i], batch, seq)
        o_all = fox_call(z1, drow, batch, seq)
        o_all = mla_call(qm, kvb, z3, HB + 2, o_all, batch, seq)
        o_all = band_call(z1, band_bias_call(rel_bias[i]), o_all, batch, seq)
        o_all = dsa_call(z2, z3, tail, o_all, batch, seq)
        merged = merge_call(u, o_all, w_gate, b_gate[i].reshape(-1, 1, d), w_branch, i)
        h, u = rowres_call(merged, w_out, i, h, g_mix_post[i], g_ffn2_pre[i], 1.0)

        a = swiglu_call(u, w2_gate, w2_up, i)
        h, u = rowres_call(a, w2_down, i, h, g_ffn2_post[i], g_ple_pre[i], 0.5)

        last = i == depth - 1
        g_next = g_ple_pre[i] if last else g_ffn1_pre[i + 1]
        h, u = rowres_call(u, w_ple_gate, i, h, g_ple_post[i], g_next, 1.0,
                           ple=(p[i].reshape(t, -1), w_ple[i].astype(CDT)), emit_u=not last)
    return h.reshape(batch, seq, d)
```

```python
import functools

import numpy as np
import jax
import jax.numpy as jnp
from jax import lax
from jax.experimental import pallas as pl
from jax.experimental.pallas import tpu as pltpu

F32 = jnp.float32
I32 = jnp.int32
CDT = jnp.bfloat16
EPS = 1e-6
NEG = -1e30
ROPE_THETA = 10000.0

LANE = 128
HEAD_DIM = 128
N_HEADS = 8
HB = N_HEADS * HEAD_DIM // LANE
N_MIXERS = 4
CHUNK = 64
MLA_Q_LORA, MLA_KV_LORA, MLA_NOPE, MLA_ROPE = 768, 512, 128, 64
CH_PREV, MAX_REL = 8, 128
IDX_HEADS, IDX_DIM = 16, 64
TOPK_MAX = 256
SPLIT_SIZES = (1024, 1024, 1024, 8, 768, 512, 64, 1024, 1024, 1024, 1024, 128, 128, 1024, 64, 16)
SPLIT_NAMES = ("qa", "ka", "va", "fa", "cq", "ckv", "kr", "qc", "kc", "vc", "qd", "kd", "vd", "qi", "ki", "wi")
INT_MIN = -2147483648

VMEM_LIMIT_BYTES = 62 * 1024 * 1024


def _cparams(*sem):
    return pltpu.CompilerParams(dimension_semantics=sem, vmem_limit_bytes=VMEM_LIMIT_BYTES)


def _tile(n, pref):
    t = min(n, pref)
    assert n % t == 0, (n, pref)
    return t


def _dot(a, b):
    return jnp.dot(a, b, preferred_element_type=F32)


def _dot_nt(a, b):
    return lax.dot_general(a, b, (((1,), (1,)), ((), ())), preferred_element_type=F32)


def _rms(y, g):
    ms = jnp.mean(y * y, axis=-1, keepdims=True)
    return y * lax.rsqrt(ms + EPS) * g


def _sigmoid(x):
    return 1.0 / (1.0 + jnp.exp(-x))


def _rmsnorm_kernel(x_ref, g_ref, o_ref):
    o_ref[...] = _rms(x_ref[...], g_ref[...]).astype(o_ref.dtype)


def rmsnorm_call(x, g):
    t, d = x.shape
    tm = _tile(t, 256)
    return pl.pallas_call(
        _rmsnorm_kernel,
        grid=(t // tm,),
        in_specs=[pl.BlockSpec((tm, d), lambda i: (i, 0)), pl.BlockSpec((1, d), lambda i: (0, 0))],
        out_specs=pl.BlockSpec((tm, d), lambda i: (i, 0)),
        out_shape=jax.ShapeDtypeStruct((t, d), CDT),
        compiler_params=_cparams("parallel"),
        name="rmsnorm",
    )(x, g.reshape(1, d))


def _swiglu_kernel(a_ref, wg_ref, wu_ref, o_ref):
    a = a_ref[...]
    g = _dot(a, wg_ref[...])
    v = _dot(a, wu_ref[...])
    o_ref[...] = (g * _sigmoid(g) * v).astype(o_ref.dtype)


def swiglu_call(u, wg, wu, layer):
    t, d = u.shape
    f = wg.shape[2]
    tm, tn = _tile(t, 1024), _tile(f, 512)
    return pl.pallas_call(
        _swiglu_kernel,
        grid=(t // tm, f // tn),
        in_specs=[pl.BlockSpec((tm, d), lambda i, j: (i, 0)),
                  pl.BlockSpec((None, d, tn), lambda i, j: (layer, 0, j)),
                  pl.BlockSpec((None, d, tn), lambda i, j: (layer, 0, j))],
        out_specs=pl.BlockSpec((tm, tn), lambda i, j: (i, j)),
        out_shape=jax.ShapeDtypeStruct((t, f), CDT),
        compiler_params=_cparams("parallel", "parallel"),
        name="swiglu_up",
    )(u, wg, wu)


ROW_CHUNK = 64
ROWRES_TK = 4096
ROWRES_TN = 1024


def _rowres_unpack(refs, ple, emit_u):
    a_ref, w_ref, h_ref, gp_ref, gn_ref = refs[:5]
    pos = 5
    p_ref = wple_ref = None
    if ple:
        p_ref, wple_ref = refs[pos:pos + 2]
        pos += 2
    hout_ref = refs[pos]
    uout_ref = refs[pos + 1] if emit_u else None
    return a_ref, w_ref, h_ref, gp_ref, gn_ref, p_ref, wple_ref, hout_ref, uout_ref


def _rowres_epilogue(h_ref, gp_ref, gn_ref, p_ref, wple_ref, hout_ref, uout_ref, coef):
    tm = hout_ref.shape[0]
    rc = min(ROW_CHUNK, tm)

    def body(r, carry):
        rows = pl.ds(pl.multiple_of(r * rc, rc), rc)
        y = hout_ref[rows, :]
        if p_ref is not None:
            e = _dot(p_ref[rows, :].astype(CDT), wple_ref[...])
            y = _sigmoid(y) * e
        hn = h_ref[rows, :] + coef * _rms(y, gp_ref[...])
        hout_ref[rows, :] = hn
        if uout_ref is not None:
            uout_ref[rows, :] = _rms(hn, gn_ref[...]).astype(uout_ref.dtype)
        return carry

    lax.fori_loop(0, tm // rc, body, 0)


def _rowres_kernel(*refs, coef, nk, nj, tn, ple, emit_u):
    a_ref, w_ref, h_hbm, gp_ref, gn_ref, p_ref, wple_ref, hout_ref, uout_ref = _rowres_unpack(refs[:-2], ple, emit_u)
    h_ref, h_sem = refs[-2:]
    i, k, j = pl.program_id(0), pl.program_id(1), pl.program_id(2)
    tm = h_ref.shape[0]
    h_copy = pltpu.make_async_copy(h_hbm.at[pl.ds(pl.multiple_of(i * tm, tm), tm), :], h_ref, h_sem)

    @pl.when((k == 0) & (j == 0))
    def _():
        h_copy.start()

    cols = pl.ds(pl.multiple_of(j * tn, tn), tn)
    part = _dot(a_ref[...], w_ref[...])
    if nk == 1:
        hout_ref[:, cols] = part
    else:
        @pl.when(k == 0)
        def _():
            hout_ref[:, cols] = part

        @pl.when(k > 0)
        def _():
            hout_ref[:, cols] += part

    @pl.when((k == nk - 1) & (j == nj - 1))
    def _():
        h_copy.wait()
        _rowres_epilogue(h_ref, gp_ref, gn_ref, p_ref, wple_ref, hout_ref, uout_ref, coef)


def _rowres_resident_kernel(*refs, coef, ple, emit_u):
    a_ref, w_ref, h_ref, gp_ref, gn_ref, p_ref, wple_ref, hout_ref, uout_ref = _rowres_unpack(refs, ple, emit_u)
    a = a_ref[...]
    d = w_ref.shape[1]
    tn = min(RESIDENT_TN, d)
    if ple:
        pe = p_ref[...].astype(CDT)
    for c in range(0, d, tn):
        y = _dot(a, w_ref[:, c:c + tn])
        if ple:
            y = _sigmoid(y) * _dot(pe, wple_ref[:, c:c + tn])
        hout_ref[:, c:c + tn] = y
    _rowres_epilogue(h_ref, gp_ref, gn_ref, None, None, hout_ref, uout_ref, coef)


RESIDENT_W_BYTES = 32 * 1024 * 1024
RESIDENT_TM = 256
RESIDENT_TN = 512


def _rowres_resident_call(a, w, layer, h, g_post, g_next, coef, ple, emit_u):
    t, kdim = a.shape
    d = w.shape[2]
    tm = _tile(t, RESIDENT_TM)
    once = pl.Buffered(1)
    in_specs = [pl.BlockSpec((tm, kdim), lambda i: (i, 0)),
                pl.BlockSpec((None, kdim, d), lambda i: (layer, 0, 0), pipeline_mode=once),
                pl.BlockSpec((tm, d), lambda i: (i, 0)),
                pl.BlockSpec((1, d), lambda i: (0, 0), pipeline_mode=once),
                pl.BlockSpec((1, d), lambda i: (0, 0), pipeline_mode=once)]
    args = [a, w, h, g_post.reshape(1, d), g_next.reshape(1, d)]
    if ple is not None:
        p, wple = ple
        in_specs += [pl.BlockSpec((tm, p.shape[1]), lambda i: (i, 0)),
                     pl.BlockSpec(wple.shape, lambda i: (0, 0), pipeline_mode=once)]
        args += [p, wple]
    out_specs = [pl.BlockSpec((tm, d), lambda i: (i, 0))]
    out_shape = [jax.ShapeDtypeStruct((t, d), F32)]
    if emit_u:
        out_specs.append(pl.BlockSpec((tm, d), lambda i: (i, 0)))
        out_shape.append(jax.ShapeDtypeStruct((t, d), CDT))
    res = pl.pallas_call(
        functools.partial(_rowres_resident_kernel, coef=coef, ple=ple is not None, emit_u=emit_u),
        grid=(t // tm,),
        in_specs=in_specs,
        out_specs=out_specs,
        out_shape=out_shape,
        compiler_params=_cparams("arbitrary"),
        name="rowres_ple" if ple is not None else "rowres_res",
    )(*args)
    return (res[0], res[1]) if emit_u else (res[0], None)


def rowres_call(a, w, layer, h, g_post, g_next, coef, ple=None, emit_u=True):
    t, kdim = a.shape
    d = w.shape[2]
    if kdim * d * jnp.dtype(w.dtype).itemsize <= RESIDENT_W_BYTES:
        return _rowres_resident_call(a, w, layer, h, g_post, g_next, coef, ple, emit_u)
    tm, tk, tn = _tile(t, 512), _tile(kdim, ROWRES_TK), _tile(d, ROWRES_TN)
    nk, nj = kdim // tk, d // tn
    once = pl.Buffered(1)
    in_specs = [pl.BlockSpec((tm, tk), lambda i, k, j: (i, k)),
                pl.BlockSpec((None, tk, tn), lambda i, k, j: (layer, k, j)),
                pl.BlockSpec(memory_space=pl.ANY),
                pl.BlockSpec((1, d), lambda i, k, j: (0, 0), pipeline_mode=once),
                pl.BlockSpec((1, d), lambda i, k, j: (0, 0), pipeline_mode=once)]
    args = [a, w, h, g_post.reshape(1, d), g_next.reshape(1, d)]
    if ple is not None:
        p, wple = ple
        in_specs += [pl.BlockSpec((tm, p.shape[1]), lambda i, k, j: (i, 0), pipeline_mode=once),
                     pl.BlockSpec(wple.shape, lambda i, k, j: (0, 0), pipeline_mode=once)]
        args += [p, wple]
    out_specs = [pl.BlockSpec((tm, d), lambda i, k, j: (i, 0))]
    out_shape = [jax.ShapeDtypeStruct((t, d), F32)]
    if emit_u:
        out_specs.append(pl.BlockSpec((tm, d), lambda i, k, j: (i, 0)))
        out_shape.append(jax.ShapeDtypeStruct((t, d), CDT))
    res = pl.pallas_call(
        functools.partial(_rowres_kernel, coef=coef, nk=nk, nj=nj, tn=tn, ple=ple is not None, emit_u=emit_u),
        grid=(t // tm, nk, nj),
        in_specs=in_specs,
        out_specs=out_specs,
        out_shape=out_shape,
        scratch_shapes=[pltpu.VMEM((tm, d), F32), pltpu.SemaphoreType.DMA(())],
        compiler_params=_cparams("arbitrary", "arbitrary", "arbitrary"),
        name="rowres_ple" if ple is not None else "rowres",
    )(*args)
    return (res[0], res[1]) if emit_u else (res[0], None)


def _mm_kernel(a_ref, w_ref, o_ref):
    acc = _dot(a_ref[...], w_ref[...])
    for blk in range(o_ref.shape[0]):
        o_ref[blk] = acc[:, blk * LANE:(blk + 1) * LANE].astype(o_ref.dtype)


def mm_call(a, w):
    t, d = a.shape
    n = w.shape[1]
    tm, tn = _tile(t, 1024), _tile(n, 1024)
    return pl.pallas_call(
        _mm_kernel,
        grid=(t // tm, n // tn),
        in_specs=[pl.BlockSpec((tm, d), lambda i, j: (i, 0)), pl.BlockSpec((d, tn), lambda i, j: (0, j))],
        out_specs=pl.BlockSpec((tn // LANE, tm, LANE), lambda i, j: (j, i, 0)),
        out_shape=jax.ShapeDtypeStruct((n // LANE, t, LANE), CDT),
        compiler_params=_cparams("parallel", "parallel"),
        name="proj_plain",
    )(a, w)


def _rope_tables(seq, half):
    inv = ROPE_THETA ** (-jnp.arange(half, dtype=F32) / half)
    ang = jnp.arange(seq, dtype=F32)[:, None] * inv[None, :]
    cos, sin = jnp.cos(ang), jnp.sin(ang)
    reps = LANE // (2 * half)
    return jnp.tile(jnp.concatenate([cos, cos], -1), (1, reps)), jnp.tile(jnp.concatenate([-sin, sin], -1), (1, reps))


def _rope_block(x, c, s, half):
    if 2 * half == LANE:
        sw = pltpu.roll(x, half, axis=1)
    else:
        lane = lax.broadcasted_iota(I32, x.shape, 1)
        first = (lane & (2 * half - 1)) < half
        sw = jnp.where(first, pltpu.roll(x, LANE - half, axis=1), pltpu.roll(x, half, axis=1))
    return x * c + sw * s


def _proj_rope_kernel(a_ref, w_ref, c_ref, s_ref, o_ref, *, half, rope_blocks):
    acc = _dot(a_ref[...], w_ref[...])
    c, s = c_ref[...], s_ref[...]
    for blk in range(o_ref.shape[0]):
        x = acc[:, blk * LANE:(blk + 1) * LANE]
        if blk in rope_blocks:
            x = _rope_block(x, c, s, half)
        o_ref[blk] = x.astype(o_ref.dtype)


def proj_rope_call(u, w, cos, sin, seq, half, rope_blocks):
    t, d = u.shape
    n = w.shape[1]
    tm = _tile(seq, 512)
    ns = seq // tm
    return pl.pallas_call(
        functools.partial(_proj_rope_kernel, half=half, rope_blocks=tuple(rope_blocks)),
        grid=(t // tm,),
        in_specs=[pl.BlockSpec((tm, d), lambda i: (i, 0)),
                  pl.BlockSpec((d, n), lambda i: (0, 0)),
                  pl.BlockSpec((tm, LANE), lambda i: (i % ns, 0)),
                  pl.BlockSpec((tm, LANE), lambda i: (i % ns, 0))],
        out_specs=pl.BlockSpec((n // LANE, tm, LANE), lambda i: (0, i, 0)),
        out_shape=jax.ShapeDtypeStruct((n // LANE, t, LANE), CDT),
        compiler_params=_cparams("parallel"),
        name=f"proj_rope{2 * half}",
    )(u, w, cos, sin)


def _proj_mla_kernel(a_ref, w_ref, gq_ref, gkv_ref, wuq_ref, wukv_ref, c_ref, s_ref, q_ref, kv_ref):
    acc = _dot(a_ref[...], w_ref[...])
    cq = _rms(acc[:, :MLA_Q_LORA], gq_ref[...]).astype(CDT)
    ckv = _rms(acc[:, MLA_Q_LORA:MLA_Q_LORA + MLA_KV_LORA], gkv_ref[...]).astype(CDT)
    qb = _dot(cq, wuq_ref[...])
    c, s = c_ref[...], s_ref[...]
    for blk in range(q_ref.shape[0]):
        x = qb[:, blk * LANE:(blk + 1) * LANE]
        if blk % 2 == 1:
            x = _rope_block(x, c, s, MLA_ROPE // 2)
        q_ref[blk] = x.astype(q_ref.dtype)
    kvb = _dot(ckv, wukv_ref[...])
    for blk in range(kv_ref.shape[0]):
        kv_ref[blk] = kvb[:, blk * LANE:(blk + 1) * LANE].astype(kv_ref.dtype)


def proj_mla_call(u, w, gq, gkv, wuq, wukv, cos, sin, seq):
    t, d = u.shape
    n = w.shape[1]
    tm = _tile(seq, 512)
    ns = seq // tm
    nq, nkv = wuq.shape[1] // LANE, wukv.shape[1] // LANE
    return pl.pallas_call(
        _proj_mla_kernel,
        grid=(t // tm,),
        in_specs=[pl.BlockSpec((tm, d), lambda i: (i, 0)),
                  pl.BlockSpec((d, n), lambda i: (0, 0)),
                  pl.BlockSpec((1, MLA_Q_LORA), lambda i: (0, 0)),
                  pl.BlockSpec((1, MLA_KV_LORA), lambda i: (0, 0)),
                  pl.BlockSpec(wuq.shape, lambda i: (0, 0)),
                  pl.BlockSpec(wukv.shape, lambda i: (0, 0)),
                  pl.BlockSpec((tm, LANE), lambda i: (i % ns, 0)),
                  pl.BlockSpec((tm, LANE), lambda i: (i % ns, 0))],
        out_specs=[pl.BlockSpec((nq, tm, LANE), lambda i: (0, i, 0)), pl.BlockSpec((nkv, tm, LANE), lambda i: (0, i, 0))],
        out_shape=[jax.ShapeDtypeStruct((nq, t, LANE), CDT), jax.ShapeDtypeStruct((nkv, t, LANE), CDT)],
        compiler_params=_cparams("parallel"),
        name="proj_mla",
    )(u, w, gq.reshape(1, -1), gkv.reshape(1, -1), wuq, wukv, cos, sin)


TAIL_ROWS = 32


def _proj_tail_kernel(a_ref, w_ref, wt_ref, o_ref, ot_ref):
    a = a_ref[...]
    o_ref[...] = _dot(a, w_ref[...])
    ot_ref[0] = _dot_nt(wt_ref[...], a)


def proj_tail_call(u, w, wt, batch, seq):
    t, d = u.shape
    tm = _tile(seq, 512)
    ns = seq // tm
    return pl.pallas_call(
        _proj_tail_kernel,
        grid=(t // tm,),
        in_specs=[pl.BlockSpec((tm, d), lambda i: (i, 0)),
                  pl.BlockSpec((d, LANE), lambda i: (0, 0)),
                  pl.BlockSpec((TAIL_ROWS, d), lambda i: (0, 0))],
        out_specs=[pl.BlockSpec((tm, LANE), lambda i: (i, 0)),
                   pl.BlockSpec((1, TAIL_ROWS, tm), lambda i: (i // ns, 0, i % ns))],
        out_shape=[jax.ShapeDtypeStruct((t, LANE), F32), jax.ShapeDtypeStruct((batch, TAIL_ROWS, seq), F32)],
        compiler_params=_cparams("parallel"),
        name="proj_tail",
    )(u, w, wt)


def _log_sigmoid(x):
    return jnp.minimum(x, 0.0) - jnp.log1p(jnp.exp(-jnp.abs(x)))


def _decay_kernel(tailt_ref, brow_ref, drow_ref):
    y = _log_sigmoid(tailt_ref[0, :N_HEADS, :] + brow_ref[...])
    seq = y.shape[1]
    idy = lax.broadcasted_iota(I32, y.shape, 1)
    d = 1
    while d < seq:
        y = y + jnp.where(idy >= d, pltpu.roll(y, d, axis=1), 0.0)
        d *= 2
    drow_ref[0] = y


def decay_call(tailt, b_f, batch, seq):
    return pl.pallas_call(
        _decay_kernel,
        grid=(batch,),
        in_specs=[pl.BlockSpec((1, TAIL_ROWS, seq), lambda b: (b, 0, 0)),
                  pl.BlockSpec((N_HEADS, 1), lambda b: (0, 0))],
        out_specs=pl.BlockSpec((1, N_HEADS, seq), lambda b: (b, 0, 0)),
        out_shape=jax.ShapeDtypeStruct((batch, N_HEADS, seq), F32),
        compiler_params=_cparams("parallel"),
        name="decay_scan",
    )(tailt, b_f.reshape(N_HEADS, 1))


ATT_TQ = 512
ATT_SUB = 2


def _prefix_attention(i, tq, nq, logits_fn, v_fn, allowed_fn, o_ref):
    sq = tq // ATT_SUB
    row = lax.broadcasted_iota(I32, (sq, sq), 0)
    col = lax.broadcasted_iota(I32, (sq, sq), 1)
    for n in range(nq):
        @pl.when(i == n)
        def _(n=n):
            for r in range(ATT_SUB):
                pre = n * tq + r * sq
                sd = jnp.where(allowed_fn(row, col), logits_fn(r, pre, sq), NEG)
                m = jnp.max(sd, axis=-1, keepdims=True)
                if pre:
                    sp = logits_fn(r, 0, pre)
                    m = jnp.maximum(m, jnp.max(sp, axis=-1, keepdims=True))
                pd = jnp.exp(sd - m)
                l = jnp.sum(pd, axis=-1, keepdims=True)
                acc = _dot(pd.astype(CDT), v_fn(pre, sq))
                if pre:
                    pp = jnp.exp(sp - m)
                    l = l + jnp.sum(pp, axis=-1, keepdims=True)
                    acc = acc + _dot(pp.astype(CDT), v_fn(0, pre))
                o_ref[r * sq:(r + 1) * sq, :] = (acc / l).astype(o_ref.dtype)


def _fox_kernel(q_ref, k_ref, v_ref, drow_ref, o_ref, *, tq, nq, scale):
    sq = tq // ATT_SUB
    qs = [q_ref[0, r * sq:(r + 1) * sq, :] for r in range(ATT_SUB)]

    def logits(r, start, n):
        return _dot_nt(qs[r], k_ref[0, start:start + n, :]) * scale - drow_ref[0, :, start:start + n]

    o_ref[1:] = jnp.zeros((N_MIXERS - 1,) + o_ref.shape[1:], o_ref.dtype)
    _prefix_attention(pl.program_id(2), tq, nq, logits, lambda start, n: v_ref[0, start:start + n, :],
                      lambda row, col: col <= row, o_ref.at[0])


def _mla_kernel(qn_ref, qr_ref, kn_ref, kr_ref, v_ref, oin_ref, o_ref, *, tq, nq, scale):
    del oin_ref
    sq = tq // ATT_SUB
    qs = [jnp.concatenate([qn_ref[0, r * sq:(r + 1) * sq, :], qr_ref[0, r * sq:(r + 1) * sq, :]], axis=1)
          for r in range(ATT_SUB)]

    def logits(r, start, n):
        k = jnp.concatenate([kn_ref[0, start:start + n, :], kr_ref[0, start:start + n, :]], axis=1)
        return _dot_nt(qs[r], k) * scale

    _prefix_attention(pl.program_id(2), tq, nq, logits, lambda start, n: v_ref[0, start:start + n, :],
                      lambda row, col: (col >> 6) <= (row >> 6), o_ref)


def fox_call(z1, drow, batch, seq):
    tq = _tile(seq, ATT_TQ)
    nq = seq // tq
    drow = drow.reshape(batch * N_HEADS, 1, seq)
    t = batch * seq
    return pl.pallas_call(
        functools.partial(_fox_kernel, tq=tq, nq=nq, scale=HEAD_DIM ** -0.5),
        grid=(batch, N_HEADS, nq),
        in_specs=[pl.BlockSpec((1, tq, LANE), lambda b, h, i: (h, b * nq + i, 0)),
                  pl.BlockSpec((1, seq, LANE), lambda b, h, i: (HB + h, b, 0)),
                  pl.BlockSpec((1, seq, LANE), lambda b, h, i: (2 * HB + h, b, 0)),
                  pl.BlockSpec((1, 1, seq), lambda b, h, i: (b * N_HEADS + h, 0, 0))],
        out_specs=pl.BlockSpec((N_MIXERS, tq, LANE), lambda b, h, i: (0, b * nq + i, h)),
        out_shape=jax.ShapeDtypeStruct((N_MIXERS, t, HB * LANE), CDT),
        compiler_params=_cparams("parallel", "parallel", "parallel"),
        name="attn_fox",
    )(z1, z1, z1, drow)


def mla_call(qm, kvb, z3, kr_block, o_all, batch, seq):
    tq = _tile(seq, ATT_TQ)
    nq = seq // tq
    return pl.pallas_call(
        functools.partial(_mla_kernel, tq=tq, nq=nq, scale=(MLA_NOPE + MLA_ROPE) ** -0.5),
        grid=(batch, N_HEADS, nq),
        in_specs=[pl.BlockSpec((1, tq, LANE), lambda b, h, i: (2 * h, b * nq + i, 0)),
                  pl.BlockSpec((1, tq, LANE), lambda b, h, i: (2 * h + 1, b * nq + i, 0)),
                  pl.BlockSpec((1, seq, LANE), lambda b, h, i: (2 * h, b, 0)),
                  pl.BlockSpec((1, seq, LANE), lambda b, h, i: (kr_block, b, 0)),
                  pl.BlockSpec((1, seq, LANE), lambda b, h, i: (2 * h + 1, b, 0)),
                  pl.BlockSpec(memory_space=pl.ANY)],
        out_specs=pl.BlockSpec((None, tq, LANE), lambda b, h, i: (1, b * nq + i, h)),
        out_shape=jax.ShapeDtypeStruct(o_all.shape, o_all.dtype),
        input_output_aliases={5: 0},
        compiler_params=_cparams("parallel", "parallel", "parallel"),
        name="attn_mla",
    )(qm, qm, kvb, z3, kvb, o_all)


BAND_TQ = 256
BAND_NKB = (CH_PREV * CHUNK) // BAND_TQ + 1
BIAS_STRIP = 128


def _band_bias_kernel(tab_ref, o_ref):
    h = pl.program_id(0)
    back = (BAND_NKB - 1) * BAND_TQ
    for kb in range(BAND_NKB):
        for q0 in range(0, BAND_TQ, BIAS_STRIP):
            for k0 in range(0, BAND_TQ, BIAS_STRIP):
                q = lax.broadcasted_iota(I32, (BIAS_STRIP, BIAS_STRIP), 0) + q0
                koff = lax.broadcasted_iota(I32, (BIAS_STRIP, BIAS_STRIP), 1) + (kb * BAND_TQ + k0 - back)
                rel = jnp.clip(q - koff, -MAX_REL, MAX_REL) + MAX_REL
                koff0 = kb * BAND_TQ + k0 - back
                lo = int(np.clip(q0 - (koff0 + BIAS_STRIP - 1), -MAX_REL, MAX_REL)) + MAX_REL
                hi = int(np.clip(q0 + BIAS_STRIP - 1 - koff0, -MAX_REL, MAX_REL)) + MAX_REL

                def body(r, acc, rel=rel):
                    return jnp.where(rel == r, tab_ref[h, r], acc)

                acc = lax.fori_loop(lo, hi + 1, body, jnp.zeros((BIAS_STRIP, BIAS_STRIP), F32))
                kc, qc = koff >> 6, q >> 6
                band = (kc >= qc - CH_PREV) & (kc <= qc)
                o_ref[0, kb, q0:q0 + BIAS_STRIP, k0:k0 + BIAS_STRIP] = jnp.where(band, acc, NEG)


def band_bias_call(rel_table):
    return pl.pallas_call(
        _band_bias_kernel,
        grid=(N_HEADS,),
        in_specs=[pl.BlockSpec(memory_space=pltpu.SMEM)],
        out_specs=pl.BlockSpec((1, BAND_NKB, BAND_TQ, BAND_TQ), lambda h: (h, 0, 0, 0)),
        out_shape=jax.ShapeDtypeStruct((N_HEADS, BAND_NKB, BAND_TQ, BAND_TQ), F32),
        compiler_params=_cparams("parallel"),
        name="band_bias",
    )(rel_table.astype(F32))


def _band_kernel(q_ref, k_ref, v_ref, bias_ref, oin_ref, o_ref, *, tq, scale):
    del oin_ref
    i = pl.program_id(1)
    for h in range(N_HEADS):
        q = q_ref[h]
        ss, vs = [], []
        for kb in range(BAND_NKB):
            kt = i - (BAND_NKB - 1) + kb
            ks = pl.ds(pl.multiple_of(jnp.maximum(kt, 0) * tq, tq), tq)
            s = _dot_nt(q, k_ref[h, ks, :]) * scale + bias_ref[h, kb]
            if kb < BAND_NKB - 1:
                s = jnp.where(kt >= 0, s, NEG)
            ss.append(s)
            vs.append(v_ref[h, ks, :])
        m = ss[0].max(axis=-1, keepdims=True)
        for s in ss[1:]:
            m = jnp.maximum(m, s.max(axis=-1, keepdims=True))
        l = jnp.zeros((tq, 1), F32)
        acc = jnp.zeros((tq, HEAD_DIM), F32)
        for s, v in zip(ss, vs):
            p = jnp.exp(s - m)
            l = l + p.sum(axis=-1, keepdims=True)
            acc = acc + _dot(p.astype(CDT), v)
        o_ref[:, h * HEAD_DIM:(h + 1) * HEAD_DIM] = (acc / l).astype(o_ref.dtype)


def band_call(z1, bias_tiles, o_all, batch, seq):
    tq = BAND_TQ
    assert seq % tq == 0
    nq = seq // tq
    return pl.pallas_call(
        functools.partial(_band_kernel, tq=tq, scale=HEAD_DIM ** -0.5),
        grid=(batch, nq),
        in_specs=[pl.BlockSpec((HB, tq, LANE), lambda b, i: (3, b * nq + i, 0)),
                  pl.BlockSpec((HB, seq, LANE), lambda b, i: (4, b, 0)),
                  pl.BlockSpec((HB, seq, LANE), lambda b, i: (5, b, 0)),
                  pl.BlockSpec(bias_tiles.shape, lambda b, i: (0, 0, 0, 0)),
                  pl.BlockSpec(memory_space=pl.ANY)],
        out_specs=pl.BlockSpec((None, tq, HB * LANE), lambda b, i: (2, b * nq + i, 0)),
        out_shape=jax.ShapeDtypeStruct(o_all.shape, o_all.dtype),
        input_output_aliases={4: 0},
        compiler_params=_cparams("parallel", "parallel"),
        name="attn_band",
    )(z1, z1, z1, bias_tiles, o_all)


DSA_TQ = 256
DSA_KSTEP = 512
DSA_HGROUP = 2


def _count(mask):
    return jnp.sum(mask.astype(F32), axis=-1, keepdims=True)


def _dsa_body(qd_ref, kd_ref, vd_ref, qi_ref, ka_ref, kb_ref, wi_ref, o_ref, *, i, tq, ext, k_sel, scale):
    assert IDX_DIM == 64
    w = (wi_ref[...] * (IDX_HEADS ** -0.5)) * (IDX_DIM ** -0.5)
    ka, kb = ka_ref[0, :ext, :], kb_ref[0, :ext, :]
    score = jnp.zeros((tq, ext), F32)
    for p in range(IDX_HEADS // 2):
        qp = qi_ref[p]
        l0 = _dot_nt(qp, ka)
        l1 = _dot_nt(qp, kb)
        c0 = N_HEADS + 2 * p
        score = score + jnp.maximum(l0, 0.0) * w[:, c0:c0 + 1] + jnp.maximum(l1, 0.0) * w[:, c0 + 1:c0 + 2]
    score = score + 0.0
    col = lax.broadcasted_iota(I32, (tq, ext), 1)
    tpos = lax.broadcasted_iota(I32, (tq, 1), 0) + i * tq
    limit = ((tpos >> 6) + 1) << 6
    valid = col < limit
    bits = lax.bitcast_convert_type(score, I32)
    key = jnp.where(bits < 0, bits ^ 0x7FFFFFFF, bits)
    key = jnp.where(valid, key, INT_MIN)

    def thr_body(it, tu):
        cand = tu | lax.shift_left(jnp.int32(1), 31 - it)
        cnt = _count(key >= (cand ^ INT_MIN))
        return jnp.where(cnt >= k_sel, cand, tu)

    tu = lax.fori_loop(0, 32, thr_body, jnp.zeros((tq, 1), I32))
    thr = tu ^ INT_MIN
    gt = key > thr
    ties = (key == thr) & valid
    n_gt = _count(gt)
    need = k_sel - n_gt
    excess = jnp.max(n_gt + _count(ties)) > k_sel

    nbits = max(1, int(np.ceil(np.log2(ext))))

    def tie_search():
        def tie_body(it, j):
            cand = j | lax.shift_left(jnp.int32(1), nbits - 1 - it)
            cnt = _count(ties & (col < cand))
            return jnp.where(cnt < need, cand, j)

        return lax.fori_loop(0, nbits, tie_body, jnp.zeros((tq, 1), I32))

    j = lax.cond(excess, tie_search, lambda: jnp.full((tq, 1), ext, I32))
    sel = gt | (ties & (col <= j))
    maskb = jnp.where(sel, 0.0, NEG)

    kd, vd = kd_ref[0, :ext, :], vd_ref[0, :ext, :]
    for g in range(N_HEADS // DSA_HGROUP):
        q = qd_ref[g * DSA_HGROUP:(g + 1) * DSA_HGROUP].reshape(DSA_HGROUP * tq, HEAD_DIM)
        s = _dot_nt(q, kd) * scale + jnp.tile(maskb, (DSA_HGROUP, 1))
        m = jnp.max(s, axis=-1, keepdims=True)
        p = jnp.exp(s - m)
        l = jnp.sum(p, axis=-1, keepdims=True)
        o = _dot(p.astype(CDT), vd) / l
        for hh in range(DSA_HGROUP):
            h = g * DSA_HGROUP + hh
            o_ref[:, h * HEAD_DIM:(h + 1) * HEAD_DIM] = o[hh * tq:(hh + 1) * tq, :].astype(o_ref.dtype)


def _dsa_kernel(qd_ref, kd_ref, vd_ref, qi_ref, ka_ref, kb_ref, wi_ref, oin_ref, o_ref, *, tq, seq, kstep, k_sel, scale):
    del oin_ref
    i = pl.program_id(1)
    nsteps = ((i + 1) * tq - 1) // kstep + 1
    for n in range(1, seq // kstep + 1):
        @pl.when(nsteps == n)
        def _(n=n):
            _dsa_body(qd_ref, kd_ref, vd_ref, qi_ref, ka_ref, kb_ref, wi_ref, o_ref,
                      i=i, tq=tq, ext=n * kstep, k_sel=k_sel, scale=scale)


def dsa_call(z2, z3, tail, o_all, batch, seq):
    tq = _tile(seq, DSA_TQ)
    nq = seq // tq
    kstep = _tile(seq, DSA_KSTEP)
    k_sel = min(TOPK_MAX, seq // 4)
    return pl.pallas_call(
        functools.partial(_dsa_kernel, tq=tq, seq=seq, kstep=kstep, k_sel=k_sel, scale=HEAD_DIM ** -0.5),
        grid=(batch, nq),
        in_specs=[pl.BlockSpec((HB, tq, LANE), lambda b, i: (0, b * nq + i, 0)),
                  pl.BlockSpec((1, seq, LANE), lambda b, i: (HB, b, 0)),
                  pl.BlockSpec((1, seq, LANE), lambda b, i: (HB + 1, b, 0)),
                  pl.BlockSpec((HB, tq, LANE), lambda b, i: (0, b * nq + i, 0)),
                  pl.BlockSpec((1, seq, LANE), lambda b, i: (HB, b, 0)),
                  pl.BlockSpec((1, seq, LANE), lambda b, i: (HB + 1, b, 0)),
                  pl.BlockSpec((tq, LANE), lambda b, i: (b * nq + i, 0)),
                  pl.BlockSpec(memory_space=pl.ANY)],
        out_specs=pl.BlockSpec((None, tq, HB * LANE), lambda b, i: (3, b * nq + i, 0)),
        out_shape=jax.ShapeDtypeStruct(o_all.shape, o_all.dtype),
        input_output_aliases={7: 0},
        compiler_params=_cparams("parallel", "parallel"),
        name="attn_dsa",
    )(z2, z2, z2, z3, z3, z3, tail, o_all)


def _merge_kernel(u_ref, o_ref, wg_ref, bg_ref, wb_ref, out_ref, acc_ref, *, nbranch):
    n = pl.program_id(2)
    g = _dot(u_ref[...], wg_ref[0]) + bg_ref[0]
    term = _sigmoid(g) * _dot(o_ref[...], wb_ref[0])

    @pl.when(n == 0)
    def _():
        acc_ref[...] = term

    @pl.when(n > 0)
    def _():
        acc_ref[...] += term

    @pl.when(n == nbranch - 1)
    def _():
        out_ref[...] = acc_ref[...].astype(out_ref.dtype)


def merge_call(u, o_all, wg, bg, wb, layer):
    t, d = u.shape
    nbranch, bw = wb.shape[1], wb.shape[2]
    tm, tn = _tile(t, 1024), _tile(d, 1024)
    return pl.pallas_call(
        functools.partial(_merge_kernel, nbranch=nbranch),
        grid=(t // tm, d // tn, nbranch),
        in_specs=[pl.BlockSpec((tm, d), lambda i, j, n: (i, 0)),
                  pl.BlockSpec((None, tm, bw), lambda i, j, n: (n, i, 0)),
                  pl.BlockSpec((None, 1, d, tn), lambda i, j, n: (layer, n, 0, j)),
                  pl.BlockSpec((1, 1, tn), lambda i, j, n: (n, 0, j)),
                  pl.BlockSpec((None, 1, bw, tn), lambda i, j, n: (layer, n, 0, j))],
        out_specs=pl.BlockSpec((tm, tn), lambda i, j, n: (i, j)),
        out_shape=jax.ShapeDtypeStruct((t, d), CDT),
        scratch_shapes=[pltpu.VMEM((tm, tn), F32)],
        compiler_params=_cparams("parallel", "parallel", "arbitrary"),
        name="gated_merge",
    )(u, o_all, wg, bg, wb)


def _pack_w_in(w_in):
    off = np.concatenate([[0], np.cumsum(SPLIT_SIZES)])
    col = {n: w_in[:, off[k]:off[k + 1]] for k, n in enumerate(SPLIT_NAMES)}
    d = w_in.shape[0]
    z64 = jnp.zeros((d, 64), w_in.dtype)
    w1 = jnp.concatenate([col[n] for n in ("qa", "ka", "va", "qc", "kc", "vc")], axis=1)
    w2 = jnp.concatenate([col["qd"], col["kd"], col["vd"]], axis=1)
    w3 = jnp.concatenate([col["qi"], col["ki"], z64, z64, col["ki"], col["kr"], z64], axis=1)
    w4 = jnp.concatenate([col["cq"], col["ckv"]], axis=1)
    w5 = jnp.concatenate([col["fa"], col["wi"], jnp.zeros((d, LANE - 24), w_in.dtype)], axis=1)
    w5t = jnp.concatenate([w5[:, :24], jnp.zeros((d, TAIL_ROWS - 24), w_in.dtype)], axis=1).T
    return [w.astype(CDT) for w in (w1, w2, w3, w4, w5, w5t)]


def _pack_w_uq(w_uq):
    r = w_uq.shape[0]
    w = w_uq.reshape(r, N_HEADS, MLA_NOPE + MLA_ROPE)
    w = jnp.concatenate([w, jnp.zeros((r, N_HEADS, 2 * LANE - MLA_NOPE - MLA_ROPE), w_uq.dtype)], axis=-1)
    return w.reshape(r, N_HEADS * 2 * LANE).astype(CDT)


def kernel(x, p, w_in, b_f, g_cq, g_ckv, w_uq, w_ukv, rel_bias, w_branch, w_gate, b_gate, w_out, w1_gate, w1_up, w1_down, w2_gate, w2_up, w2_down, g_ffn1_pre, g_ffn1_post, g_mix_pre, g_mix_post, g_ffn2_pre, g_ffn2_post, g_ple_pre, g_ple_post, w_ple, w_ple_gate):
    batch, seq, d = x.shape
    depth = w_in.shape[0]
    t = batch * seq
    cos128, sin128 = _rope_tables(seq, HEAD_DIM // 2)
    cos64, sin64 = _rope_tables(seq, IDX_DIM // 2)

    w1_gate, w1_up, w1_down, w2_gate, w2_up, w2_down, w_gate, w_branch, w_out, w_ple_gate = (
        w.astype(CDT) for w in (w1_gate, w1_up, w1_down, w2_gate, w2_up, w2_down, w_gate, w_branch, w_out, w_ple_gate))

    h = x.reshape(t, d)
    u = rmsnorm_call(h, g_ffn1_pre[0])
    for i in range(depth):
        a = swiglu_call(u, w1_gate, w1_up, i)
        h, u = rowres_call(a, w1_down, i, h, g_ffn1_post[i], g_mix_pre[i], 0.5)

        w1, w2, w3, w4, w5, w5t = _pack_w_in(w_in[i])
        z1 = mm_call(u, w1)
        z2 = proj_rope_call(u, w2, cos128, sin128, seq, HEAD_DIM // 2, range(HB + 1))
        z3 = proj_rope_call(u, w3, cos64, sin64, seq, IDX_DIM // 2, range(HB + 3))
        qm, kvb = proj_mla_call(u, w4, g_cq[i], g_ckv[i], _pack_w_uq(w_uq[i]), w_ukv[i].astype(CDT), cos64, sin64, seq)
        tail, tailt = proj_tail_call(u, w5, w5t, batch, seq)
        drow = decay_call(tailt, b_f[i], batch, seq)
        o_all = fox_call(z1, drow, batch, seq)
        o_all = mla_call(qm, kvb, z3, HB + 2, o_all, batch, seq)
        o_all = band_call(z1, band_bias_call(rel_bias[i]), o_all, batch, seq)
        o_all = dsa_call(z2, z3, tail, o_all, batch, seq)
        merged = merge_call(u, o_all, w_gate, b_gate[i].reshape(-1, 1, d), w_branch, i)
        h, u = rowres_call(merged, w_out, i, h, g_mix_post[i], g_ffn2_pre[i], 1.0)

        a = swiglu_call(u, w2_gate, w2_up, i)
        h, u = rowres_call(a, w2_down, i, h, g_ffn2_post[i], g_ple_pre[i], 0.5)

        last = i == depth - 1
        g_next = g_ple_pre[i] if last else g_ffn1_pre[i + 1]
        h, u = rowres_call(u, w_ple_gate, i, h, g_ple_post[i], g_next, 1.0,
                           ple=(p[i].reshape(t, -1), w_ple[i].astype(CDT)), emit_u=not last)
    return h.reshape(batch, seq, d)
```

```python
import functools

import numpy as np
import jax
import jax.numpy as jnp
from jax import lax
from jax.experimental import pallas as pl
from jax.experimental.pallas import tpu as pltpu

F32 = jnp.float32
I32 = jnp.int32
CDT = jnp.bfloat16
EPS = 1e-6
NEG = -1e30
ROPE_THETA = 10000.0

LANE = 128
HEAD_DIM = 128
N_HEADS = 8
HB = N_HEADS * HEAD_DIM // LANE
N_MIXERS = 4
CHUNK = 64
MLA_Q_LORA, MLA_KV_LORA, MLA_NOPE, MLA_ROPE = 768, 512, 128, 64
CH_PREV, MAX_REL = 8, 128
IDX_HEADS, IDX_DIM = 16, 64
TOPK_MAX = 256
SPLIT_SIZES = (1024, 1024, 1024, 8, 768, 512, 64, 1024, 1024, 1024, 1024, 128, 128, 1024, 64, 16)
SPLIT_NAMES = ("qa", "ka", "va", "fa", "cq", "ckv", "kr", "qc", "kc", "vc", "qd", "kd", "vd", "qi", "ki", "wi")
INT_MIN = -2147483648

VMEM_LIMIT_BYTES = 62 * 1024 * 1024


def _cparams(*sem):
    return pltpu.CompilerParams(dimension_semantics=sem, vmem_limit_bytes=VMEM_LIMIT_BYTES)


def _tile(n, pref):
    t = min(n, pref)
    assert n % t == 0, (n, pref)
    return t


def _dot(a, b):
    return jnp.dot(a, b, preferred_element_type=F32)


def _dot_nt(a, b):
    return lax.dot_general(a, b, (((1,), (1,)), ((), ())), preferred_element_type=F32)


def _rms(y, g):
    ms = jnp.mean(y * y, axis=-1, keepdims=True)
    return y * lax.rsqrt(ms + EPS) * g


def _sigmoid(x):
    return 1.0 / (1.0 + jnp.exp(-x))


def _rmsnorm_kernel(x_ref, g_ref, o_ref):
    o_ref[...] = _rms(x_ref[...], g_ref[...]).astype(o_ref.dtype)


def rmsnorm_call(x, g):
    t, d = x.shape
    tm = _tile(t, 256)
    return pl.pallas_call(
        _rmsnorm_kernel,
        grid=(t // tm,),
        in_specs=[pl.BlockSpec((tm, d), lambda i: (i, 0)), pl.BlockSpec((1, d), lambda i: (0, 0))],
        out_specs=pl.BlockSpec((tm, d), lambda i: (i, 0)),
        out_shape=jax.ShapeDtypeStruct((t, d), CDT),
        compiler_params=_cparams("parallel"),
        name="rmsnorm",
    )(x, g.reshape(1, d))


def _swiglu_kernel(a_ref, wg_ref, wu_ref, o_ref):
    a = a_ref[...]
    g = _dot(a, wg_ref[...])
    v = _dot(a, wu_ref[...])
    o_ref[...] = (g * _sigmoid(g) * v).astype(o_ref.dtype)


def swiglu_call(u, wg, wu, layer):
    t, d = u.shape
    f = wg.shape[2]
    tm, tn = _tile(t, 1024), _tile(f, 512)
    return pl.pallas_call(
        _swiglu_kernel,
        grid=(t // tm, f // tn),
        in_specs=[pl.BlockSpec((tm, d), lambda i, j: (i, 0)),
                  pl.BlockSpec((None, d, tn), lambda i, j: (layer, 0, j)),
                  pl.BlockSpec((None, d, tn), lambda i, j: (layer, 0, j))],
        out_specs=pl.BlockSpec((tm, tn), lambda i, j: (i, j)),
        out_shape=jax.ShapeDtypeStruct((t, f), CDT),
        compiler_params=_cparams("parallel", "parallel"),
        name="swiglu_up",
    )(u, wg, wu)


ROW_CHUNK = 64
ROWRES_TK = 4096
ROWRES_TN = 1024
STREAM_PARTS = 4


def _rowres_unpack(refs, ple, emit_u):
    a_ref, w_ref, h_ref, gp_ref, gn_ref = refs[:5]
    pos = 5
    p_ref = wple_ref = None
    if ple:
        p_ref, wple_ref = refs[pos:pos + 2]
        pos += 2
    hout_ref = refs[pos]
    uout_ref = refs[pos + 1] if emit_u else None
    return a_ref, w_ref, h_ref, gp_ref, gn_ref, p_ref, wple_ref, hout_ref, uout_ref


def _rowres_epilogue(h_ref, gp_ref, gn_ref, p_ref, wple_ref, hout_ref, uout_ref, coef, row0=None, nrows=None):
    tm = hout_ref.shape[0] if nrows is None else nrows
    rc = min(ROW_CHUNK, tm)

    def chunk(rows):
        y = hout_ref[rows, :]
        if p_ref is not None:
            e = _dot(p_ref[rows, :].astype(CDT), wple_ref[...])
            y = _sigmoid(y) * e
        hn = h_ref[rows, :] + coef * _rms(y, gp_ref[...])
        hout_ref[rows, :] = hn
        if uout_ref is not None:
            uout_ref[rows, :] = _rms(hn, gn_ref[...]).astype(uout_ref.dtype)

    if row0 is not None:
        for r in range(row0, row0 + nrows, rc):
            chunk(slice(r, r + rc))
        return

    def body(r, carry):
        chunk(pl.ds(pl.multiple_of(r * rc, rc), rc))
        return carry

    lax.fori_loop(0, tm // rc, body, 0)


def _rowres_kernel(*refs, coef, nk, nj, tn, ple, emit_u):
    a_ref, w_ref, h_hbm, gp_ref, gn_ref, p_ref, wple_ref, hout_ref, uout_ref = _rowres_unpack(refs[:-2], ple, emit_u)
    h_ref, h_sem = refs[-2:]
    i, k, j = pl.program_id(0), pl.program_id(1), pl.program_id(2)
    tm = h_ref.shape[0]
    h_copy = pltpu.make_async_copy(h_hbm.at[pl.ds(pl.multiple_of(i * tm, tm), tm), :], h_ref, h_sem)

    @pl.when((k == 0) & (j == 0))
    def _():
        h_copy.start()

    last = (k == nk - 1) & (j == nj - 1)

    @pl.when(jnp.logical_not(last))
    def _():
        cols = pl.ds(pl.multiple_of(j * tn, tn), tn)
        part = _dot(a_ref[...], w_ref[...])

        @pl.when(k == 0)
        def _():
            hout_ref[:, cols] = part

        @pl.when(k > 0)
        def _():
            hout_ref[:, cols] += part

    @pl.when(last)
    def _():
        h_copy.wait()
        c0 = (nj - 1) * tn
        rp = tm // STREAM_PARTS
        for row0 in range(0, tm, rp):
            rows = slice(row0, row0 + rp)
            part = _dot(a_ref[rows, :], w_ref[...])
            if nk == 1:
                hout_ref[rows, c0:c0 + tn] = part
            else:
                hout_ref[rows, c0:c0 + tn] += part
            _rowres_epilogue(h_ref, gp_ref, gn_ref, p_ref, wple_ref, hout_ref, uout_ref, coef, row0, rp)


def _rowres_resident_kernel(*refs, coef, ple, emit_u):
    a_ref, w_ref, h_ref, gp_ref, gn_ref, p_ref, wple_ref, hout_ref, uout_ref = _rowres_unpack(refs, ple, emit_u)
    tm, d = hout_ref.shape
    tn = min(RESIDENT_TN, d)
    part = tm // RESIDENT_PARTS
    for row0 in range(0, tm, part):
        rows = slice(row0, row0 + part)
        a = a_ref[rows, :]
        if ple:
            pe = p_ref[rows, :].astype(CDT)
        for c in range(0, d, tn):
            y = _dot(a, w_ref[:, c:c + tn])
            if ple:
                y = _sigmoid(y) * _dot(pe, wple_ref[:, c:c + tn])
            hout_ref[rows, c:c + tn] = y
        _rowres_epilogue(h_ref, gp_ref, gn_ref, None, None, hout_ref, uout_ref, coef, row0, part)


RESIDENT_W_BYTES = 32 * 1024 * 1024
RESIDENT_TM = 256
RESIDENT_PARTS = 2
RESIDENT_TN = 512


def _rowres_resident_call(a, w, layer, h, g_post, g_next, coef, ple, emit_u):
    t, kdim = a.shape
    d = w.shape[2]
    tm = _tile(t, RESIDENT_TM)
    once = pl.Buffered(1)
    in_specs = [pl.BlockSpec((tm, kdim), lambda i: (i, 0)),
                pl.BlockSpec((None, kdim, d), lambda i: (layer, 0, 0), pipeline_mode=once),
                pl.BlockSpec((tm, d), lambda i: (i, 0)),
                pl.BlockSpec((1, d), lambda i: (0, 0), pipeline_mode=once),
                pl.BlockSpec((1, d), lambda i: (0, 0), pipeline_mode=once)]
    args = [a, w, h, g_post.reshape(1, d), g_next.reshape(1, d)]
    if ple is not None:
        p, wple = ple
        in_specs += [pl.BlockSpec((tm, p.shape[1]), lambda i: (i, 0)),
                     pl.BlockSpec(wple.shape, lambda i: (0, 0), pipeline_mode=once)]
        args += [p, wple]
    out_specs = [pl.BlockSpec((tm, d), lambda i: (i, 0))]
    out_shape = [jax.ShapeDtypeStruct((t, d), F32)]
    if emit_u:
        out_specs.append(pl.BlockSpec((tm, d), lambda i: (i, 0)))
        out_shape.append(jax.ShapeDtypeStruct((t, d), CDT))
    res = pl.pallas_call(
        functools.partial(_rowres_resident_kernel, coef=coef, ple=ple is not None, emit_u=emit_u),
        grid=(t // tm,),
        in_specs=in_specs,
        out_specs=out_specs,
        out_shape=out_shape,
        compiler_params=_cparams("arbitrary"),
        name="rowres_ple" if ple is not None else "rowres_res",
    )(*args)
    return (res[0], res[1]) if emit_u else (res[0], None)


def rowres_call(a, w, layer, h, g_post, g_next, coef, ple=None, emit_u=True):
    t, kdim = a.shape
    d = w.shape[2]
    if kdim * d * jnp.dtype(w.dtype).itemsize <= RESIDENT_W_BYTES:
        return _rowres_resident_call(a, w, layer, h, g_post, g_next, coef, ple, emit_u)
    tm, tk, tn = _tile(t, 512), _tile(kdim, ROWRES_TK), _tile(d, ROWRES_TN)
    nk, nj = kdim // tk, d // tn
    once = pl.Buffered(1)
    in_specs = [pl.BlockSpec((tm, tk), lambda i, k, j: (i, k)),
                pl.BlockSpec((None, tk, tn), lambda i, k, j: (layer, k, j)),
                pl.BlockSpec(memory_space=pl.ANY),
                pl.BlockSpec((1, d), lambda i, k, j: (0, 0), pipeline_mode=once),
                pl.BlockSpec((1, d), lambda i, k, j: (0, 0), pipeline_mode=once)]
    args = [a, w, h, g_post.reshape(1, d), g_next.reshape(1, d)]
    if ple is not None:
        p, wple = ple
        in_specs += [pl.BlockSpec((tm, p.shape[1]), lambda i, k, j: (i, 0), pipeline_mode=once),
                     pl.BlockSpec(wple.shape, lambda i, k, j: (0, 0), pipeline_mode=once)]
        args += [p, wple]
    out_specs = [pl.BlockSpec((tm, d), lambda i, k, j: (i, 0))]
    out_shape = [jax.ShapeDtypeStruct((t, d), F32)]
    if emit_u:
        out_specs.append(pl.BlockSpec((tm, d), lambda i, k, j: (i, 0)))
        out_shape.append(jax.ShapeDtypeStruct((t, d), CDT))
    res = pl.pallas_call(
        functools.partial(_rowres_kernel, coef=coef, nk=nk, nj=nj, tn=tn, ple=ple is not None, emit_u=emit_u),
        grid=(t // tm, nk, nj),
        in_specs=in_specs,
        out_specs=out_specs,
        out_shape=out_shape,
        scratch_shapes=[pltpu.VMEM((tm, d), F32), pltpu.SemaphoreType.DMA(())],
        compiler_params=_cparams("arbitrary", "arbitrary", "arbitrary"),
        name="rowres_ple" if ple is not None else "rowres",
    )(*args)
    return (res[0], res[1]) if emit_u else (res[0], None)


def _mm_kernel(a_ref, w_ref, o_ref):
    acc = _dot(a_ref[...], w_ref[...])
    for blk in range(o_ref.shape[0]):
        o_ref[blk] = acc[:, blk * LANE:(blk + 1) * LANE].astype(o_ref.dtype)


def mm_call(a, w):
    t, d = a.shape
    n = w.shape[1]
    tm, tn = _tile(t, 1024), _tile(n, 1024)
    return pl.pallas_call(
        _mm_kernel,
        grid=(t // tm, n // tn),
        in_specs=[pl.BlockSpec((tm, d), lambda i, j: (i, 0)), pl.BlockSpec((d, tn), lambda i, j: (0, j))],
        out_specs=pl.BlockSpec((tn // LANE, tm, LANE), lambda i, j: (j, i, 0)),
        out_shape=jax.ShapeDtypeStruct((n // LANE, t, LANE), CDT),
        compiler_params=_cparams("parallel", "parallel"),
        name="proj_plain",
    )(a, w)


def _rope_tables(seq, half):
    inv = ROPE_THETA ** (-jnp.arange(half, dtype=F32) / half)
    ang = jnp.arange(seq, dtype=F32)[:, None] * inv[None, :]
    cos, sin = jnp.cos(ang), jnp.sin(ang)
    reps = LANE // (2 * half)
    return jnp.tile(jnp.concatenate([cos, cos], -1), (1, reps)), jnp.tile(jnp.concatenate([-sin, sin], -1), (1, reps))


def _rope_block(x, c, s, half):
    if 2 * half == LANE:
        sw = pltpu.roll(x, half, axis=1)
    else:
        lane = lax.broadcasted_iota(I32, x.shape, 1)
        first = (lane & (2 * half - 1)) < half
        sw = jnp.where(first, pltpu.roll(x, LANE - half, axis=1), pltpu.roll(x, half, axis=1))
    return x * c + sw * s


def _proj_rope_kernel(a_ref, w_ref, c_ref, s_ref, o_ref, *, half, rope_blocks):
    acc = _dot(a_ref[...], w_ref[...])
    c, s = c_ref[...], s_ref[...]
    for blk in range(o_ref.shape[0]):
        x = acc[:, blk * LANE:(blk + 1) * LANE]
        if blk in rope_blocks:
            x = _rope_block(x, c, s, half)
        o_ref[blk] = x.astype(o_ref.dtype)


def proj_rope_call(u, w, cos, sin, seq, half, rope_blocks):
    t, d = u.shape
    n = w.shape[1]
    tm = _tile(seq, 512)
    ns = seq // tm
    return pl.pallas_call(
        functools.partial(_proj_rope_kernel, half=half, rope_blocks=tuple(rope_blocks)),
        grid=(t // tm,),
        in_specs=[pl.BlockSpec((tm, d), lambda i: (i, 0)),
                  pl.BlockSpec((d, n), lambda i: (0, 0)),
                  pl.BlockSpec((tm, LANE), lambda i: (i % ns, 0)),
                  pl.BlockSpec((tm, LANE), lambda i: (i % ns, 0))],
        out_specs=pl.BlockSpec((n // LANE, tm, LANE), lambda i: (0, i, 0)),
        out_shape=jax.ShapeDtypeStruct((n // LANE, t, LANE), CDT),
        compiler_params=_cparams("parallel"),
        name=f"proj_rope{2 * half}",
    )(u, w, cos, sin)


def _proj_mla_kernel(a_ref, w_ref, gq_ref, gkv_ref, wuq_ref, wukv_ref, c_ref, s_ref, q_ref, kv_ref):
    acc = _dot(a_ref[...], w_ref[...])
    cq = _rms(acc[:, :MLA_Q_LORA], gq_ref[...]).astype(CDT)
    ckv = _rms(acc[:, MLA_Q_LORA:MLA_Q_LORA + MLA_KV_LORA], gkv_ref[...]).astype(CDT)
    qb = _dot(cq, wuq_ref[...])
    c, s = c_ref[...], s_ref[...]
    for blk in range(q_ref.shape[0]):
        x = qb[:, blk * LANE:(blk + 1) * LANE]
        if blk % 2 == 1:
            x = _rope_block(x, c, s, MLA_ROPE // 2)
        q_ref[blk] = x.astype(q_ref.dtype)
    kvb = _dot(ckv, wukv_ref[...])
    for blk in range(kv_ref.shape[0]):
        kv_ref[blk] = kvb[:, blk * LANE:(blk + 1) * LANE].astype(kv_ref.dtype)


def proj_mla_call(u, w, gq, gkv, wuq, wukv, cos, sin, seq):
    t, d = u.shape
    n = w.shape[1]
    tm = _tile(seq, 512)
    ns = seq // tm
    nq, nkv = wuq.shape[1] // LANE, wukv.shape[1] // LANE
    return pl.pallas_call(
        _proj_mla_kernel,
        grid=(t // tm,),
        in_specs=[pl.BlockSpec((tm, d), lambda i: (i, 0)),
                  pl.BlockSpec((d, n), lambda i: (0, 0)),
                  pl.BlockSpec((1, MLA_Q_LORA), lambda i: (0, 0)),
                  pl.BlockSpec((1, MLA_KV_LORA), lambda i: (0, 0)),
                  pl.BlockSpec(wuq.shape, lambda i: (0, 0)),
                  pl.BlockSpec(wukv.shape, lambda i: (0, 0)),
                  pl.BlockSpec((tm, LANE), lambda i: (i % ns, 0)),
                  pl.BlockSpec((tm, LANE), lambda i: (i % ns, 0))],
        out_specs=[pl.BlockSpec((nq, tm, LANE), lambda i: (0, i, 0)), pl.BlockSpec((nkv, tm, LANE), lambda i: (0, i, 0))],
        out_shape=[jax.ShapeDtypeStruct((nq, t, LANE), CDT), jax.ShapeDtypeStruct((nkv, t, LANE), CDT)],
        compiler_params=_cparams("parallel"),
        name="proj_mla",
    )(u, w, gq.reshape(1, -1), gkv.reshape(1, -1), wuq, wukv, cos, sin)


TAIL_ROWS = 32


def _proj_tail_kernel(a_ref, w_ref, wt_ref, o_ref, ot_ref):
    a = a_ref[...]
    o_ref[...] = _dot(a, w_ref[...])
    ot_ref[0] = _dot_nt(wt_ref[...], a)


def proj_tail_call(u, w, wt, batch, seq):
    t, d = u.shape
    tm = _tile(seq, 512)
    ns = seq // tm
    return pl.pallas_call(
        _proj_tail_kernel,
        grid=(t // tm,),
        in_specs=[pl.BlockSpec((tm, d), lambda i: (i, 0)),
                  pl.BlockSpec((d, LANE), lambda i: (0, 0)),
                  pl.BlockSpec((TAIL_ROWS, d), lambda i: (0, 0))],
        out_specs=[pl.BlockSpec((tm, LANE), lambda i: (i, 0)),
                   pl.BlockSpec((1, TAIL_ROWS, tm), lambda i: (i // ns, 0, i % ns))],
        out_shape=[jax.ShapeDtypeStruct((t, LANE), F32), jax.ShapeDtypeStruct((batch, TAIL_ROWS, seq), F32)],
        compiler_params=_cparams("parallel"),
        name="proj_tail",
    )(u, w, wt)


def _log_sigmoid(x):
    return jnp.minimum(x, 0.0) - jnp.log1p(jnp.exp(-jnp.abs(x)))


def _decay_kernel(tailt_ref, brow_ref, drow_ref):
    y = _log_sigmoid(tailt_ref[0, :N_HEADS, :] + brow_ref[...])
    seq = y.shape[1]
    idy = lax.broadcasted_iota(I32, y.shape, 1)
    d = 1
    while d < seq:
        y = y + jnp.where(idy >= d, pltpu.roll(y, d, axis=1), 0.0)
        d *= 2
    drow_ref[0] = y


def decay_call(tailt, b_f, batch, seq):
    return pl.pallas_call(
        _decay_kernel,
        grid=(batch,),
        in_specs=[pl.BlockSpec((1, TAIL_ROWS, seq), lambda b: (b, 0, 0)),
                  pl.BlockSpec((N_HEADS, 1), lambda b: (0, 0))],
        out_specs=pl.BlockSpec((1, N_HEADS, seq), lambda b: (b, 0, 0)),
        out_shape=jax.ShapeDtypeStruct((batch, N_HEADS, seq), F32),
        compiler_params=_cparams("parallel"),
        name="decay_scan",
    )(tailt, b_f.reshape(N_HEADS, 1))


ATT_TQ = 512
ATT_SUB = 2


def _prefix_attention(i, tq, nq, logits_fn, v_fn, allowed_fn, o_ref):
    sq = tq // ATT_SUB
    row = lax.broadcasted_iota(I32, (sq, sq), 0)
    col = lax.broadcasted_iota(I32, (sq, sq), 1)
    for n in range(nq):
        @pl.when(i == n)
        def _(n=n):
            for r in range(ATT_SUB):
                pre = n * tq + r * sq
                sd = jnp.where(allowed_fn(row, col), logits_fn(r, pre, sq), NEG)
                m = jnp.max(sd, axis=-1, keepdims=True)
                if pre:
                    sp = logits_fn(r, 0, pre)
                    m = jnp.maximum(m, jnp.max(sp, axis=-1, keepdims=True))
                pd = jnp.exp(sd - m)
                l = jnp.sum(pd, axis=-1, keepdims=True)
                acc = _dot(pd.astype(CDT), v_fn(pre, sq))
                if pre:
                    pp = jnp.exp(sp - m)
                    l = l + jnp.sum(pp, axis=-1, keepdims=True)
                    acc = acc + _dot(pp.astype(CDT), v_fn(0, pre))
                o_ref[r * sq:(r + 1) * sq, :] = (acc / l).astype(o_ref.dtype)


def _fox_kernel(q_ref, k_ref, v_ref, drow_ref, o_ref, *, tq, nq, scale):
    sq = tq // ATT_SUB
    qs = [q_ref[0, r * sq:(r + 1) * sq, :] for r in range(ATT_SUB)]

    def logits(r, start, n):
        return _dot_nt(qs[r], k_ref[0, start:start + n, :]) * scale - drow_ref[0, :, start:start + n]

    o_ref[1:] = jnp.zeros((N_MIXERS - 1,) + o_ref.shape[1:], o_ref.dtype)
    _prefix_attention(pl.program_id(2), tq, nq, logits, lambda start, n: v_ref[0, start:start + n, :],
                      lambda row, col: col <= row, o_ref.at[0])


def _mla_kernel(qn_ref, qr_ref, kn_ref, kr_ref, v_ref, oin_ref, o_ref, *, tq, nq, scale):
    del oin_ref
    sq = tq // ATT_SUB
    qs = [jnp.concatenate([qn_ref[0, r * sq:(r + 1) * sq, :], qr_ref[0, r * sq:(r + 1) * sq, :]], axis=1)
          for r in range(ATT_SUB)]

    def logits(r, start, n):
        k = jnp.concatenate([kn_ref[0, start:start + n, :], kr_ref[0, start:start + n, :]], axis=1)
        return _dot_nt(qs[r], k) * scale

    _prefix_attention(pl.program_id(2), tq, nq, logits, lambda start, n: v_ref[0, start:start + n, :],
                      lambda row, col: (col >> 6) <= (row >> 6), o_ref)


def fox_call(z1, drow, batch, seq):
    tq = _tile(seq, ATT_TQ)
    nq = seq // tq
    drow = drow.reshape(batch * N_HEADS, 1, seq)
    t = batch * seq
    return pl.pallas_call(
        functools.partial(_fox_kernel, tq=tq, nq=nq, scale=HEAD_DIM ** -0.5),
        grid=(batch, N_HEADS, nq),
        in_specs=[pl.BlockSpec((1, tq, LANE), lambda b, h, i: (h, b * nq + i, 0)),
                  pl.BlockSpec((1, seq, LANE), lambda b, h, i: (HB + h, b, 0)),
                  pl.BlockSpec((1, seq, LANE), lambda b, h, i: (2 * HB + h, b, 0)),
                  pl.BlockSpec((1, 1, seq), lambda b, h, i: (b * N_HEADS + h, 0, 0))],
        out_specs=pl.BlockSpec((N_MIXERS, tq, LANE), lambda b, h, i: (0, b * nq + i, h)),
        out_shape=jax.ShapeDtypeStruct((N_MIXERS, t, HB * LANE), CDT),
        compiler_params=_cparams("parallel", "parallel", "parallel"),
        name="attn_fox",
    )(z1, z1, z1, drow)


def mla_call(qm, kvb, z3, kr_block, o_all, batch, seq):
    tq = _tile(seq, ATT_TQ)
    nq = seq // tq
    return pl.pallas_call(
        functools.partial(_mla_kernel, tq=tq, nq=nq, scale=(MLA_NOPE + MLA_ROPE) ** -0.5),
        grid=(batch, N_HEADS, nq),
        in_specs=[pl.BlockSpec((1, tq, LANE), lambda b, h, i: (2 * h, b * nq + i, 0)),
                  pl.BlockSpec((1, tq, LANE), lambda b, h, i: (2 * h + 1, b * nq + i, 0)),
                  pl.BlockSpec((1, seq, LANE), lambda b, h, i: (2 * h, b, 0)),
                  pl.BlockSpec((1, seq, LANE), lambda b, h, i: (kr_block, b, 0)),
                  pl.BlockSpec((1, seq, LANE), lambda b, h, i: (2 * h + 1, b, 0)),
                  pl.BlockSpec(memory_space=pl.ANY)],
        out_specs=pl.BlockSpec((None, tq, LANE), lambda b, h, i: (1, b * nq + i, h)),
        out_shape=jax.ShapeDtypeStruct(o_all.shape, o_all.dtype),
        input_output_aliases={5: 0},
        compiler_params=_cparams("parallel", "parallel", "parallel"),
        name="attn_mla",
    )(qm, qm, kvb, z3, kvb, o_all)


BAND_TQ = 256
BAND_NKB = (CH_PREV * CHUNK) // BAND_TQ + 1
BIAS_STRIP = 128


def _band_bias_kernel(tab_ref, o_ref):
    h = pl.program_id(0)
    back = (BAND_NKB - 1) * BAND_TQ
    for kb in range(BAND_NKB):
        for q0 in range(0, BAND_TQ, BIAS_STRIP):
            for k0 in range(0, BAND_TQ, BIAS_STRIP):
                q = lax.broadcasted_iota(I32, (BIAS_STRIP, BIAS_STRIP), 0) + q0
                koff = lax.broadcasted_iota(I32, (BIAS_STRIP, BIAS_STRIP), 1) + (kb * BAND_TQ + k0 - back)
                rel = jnp.clip(q - koff, -MAX_REL, MAX_REL) + MAX_REL
                koff0 = kb * BAND_TQ + k0 - back
                lo = int(np.clip(q0 - (koff0 + BIAS_STRIP - 1), -MAX_REL, MAX_REL)) + MAX_REL
                hi = int(np.clip(q0 + BIAS_STRIP - 1 - koff0, -MAX_REL, MAX_REL)) + MAX_REL

                def body(r, acc, rel=rel):
                    return jnp.where(rel == r, tab_ref[h, r], acc)

                acc = lax.fori_loop(lo, hi + 1, body, jnp.zeros((BIAS_STRIP, BIAS_STRIP), F32))
                kc, qc = koff >> 6, q >> 6
                band = (kc >= qc - CH_PREV) & (kc <= qc)
                o_ref[0, kb, q0:q0 + BIAS_STRIP, k0:k0 + BIAS_STRIP] = jnp.where(band, acc, NEG)


def band_bias_call(rel_table):
    return pl.pallas_call(
        _band_bias_kernel,
        grid=(N_HEADS,),
        in_specs=[pl.BlockSpec(memory_space=pltpu.SMEM)],
        out_specs=pl.BlockSpec((1, BAND_NKB, BAND_TQ, BAND_TQ), lambda h: (h, 0, 0, 0)),
        out_shape=jax.ShapeDtypeStruct((N_HEADS, BAND_NKB, BAND_TQ, BAND_TQ), F32),
        compiler_params=_cparams("parallel"),
        name="band_bias",
    )(rel_table.astype(F32))


def _band_kernel(q_ref, k_ref, v_ref, bias_ref, oin_ref, o_ref, *, tq, scale):
    del oin_ref
    i = pl.program_id(1)
    for h in range(N_HEADS):
        q = q_ref[h]
        ss, vs = [], []
        for kb in range(BAND_NKB):
            kt = i - (BAND_NKB - 1) + kb
            ks = pl.ds(pl.multiple_of(jnp.maximum(kt, 0) * tq, tq), tq)
            s = _dot_nt(q, k_ref[h, ks, :]) * scale + bias_ref[h, kb]
            if kb < BAND_NKB - 1:
                s = jnp.where(kt >= 0, s, NEG)
            ss.append(s)
            vs.append(v_ref[h, ks, :])
        m = ss[0].max(axis=-1, keepdims=True)
        for s in ss[1:]:
            m = jnp.maximum(m, s.max(axis=-1, keepdims=True))
        l = jnp.zeros((tq, 1), F32)
        acc = jnp.zeros((tq, HEAD_DIM), F32)
        for s, v in zip(ss, vs):
            p = jnp.exp(s - m)
            l = l + p.sum(axis=-1, keepdims=True)
            acc = acc + _dot(p.astype(CDT), v)
        o_ref[:, h * HEAD_DIM:(h + 1) * HEAD_DIM] = (acc / l).astype(o_ref.dtype)


def band_call(z1, bias_tiles, o_all, batch, seq):
    tq = BAND_TQ
    assert seq % tq == 0
    nq = seq // tq
    return pl.pallas_call(
        functools.partial(_band_kernel, tq=tq, scale=HEAD_DIM ** -0.5),
        grid=(batch, nq),
        in_specs=[pl.BlockSpec((HB, tq, LANE), lambda b, i: (3, b * nq + i, 0)),
                  pl.BlockSpec((HB, seq, LANE), lambda b, i: (4, b, 0)),
                  pl.BlockSpec((HB, seq, LANE), lambda b, i: (5, b, 0)),
                  pl.BlockSpec(bias_tiles.shape, lambda b, i: (0, 0, 0, 0)),
                  pl.BlockSpec(memory_space=pl.ANY)],
        out_specs=pl.BlockSpec((None, tq, HB * LANE), lambda b, i: (2, b * nq + i, 0)),
        out_shape=jax.ShapeDtypeStruct(o_all.shape, o_all.dtype),
        input_output_aliases={4: 0},
        compiler_params=_cparams("parallel", "parallel"),
        name="attn_band",
    )(z1, z1, z1, bias_tiles, o_all)


DSA_TQ = 512
DSA_KSTEP = 512
DSA_HGROUP = 1


def _count(mask):
    return jnp.sum(mask.astype(F32), axis=-1, keepdims=True)


def _dsa_body(qd_ref, kd_ref, vd_ref, qi_ref, ka_ref, kb_ref, wi_ref, o_ref, *, i, tq, ext, k_sel, scale):
    assert IDX_DIM == 64
    w = (wi_ref[...] * (IDX_HEADS ** -0.5)) * (IDX_DIM ** -0.5)
    ka, kb = ka_ref[0, :ext, :], kb_ref[0, :ext, :]
    score = jnp.zeros((tq, ext), F32)
    for p in range(IDX_HEADS // 2):
        qp = qi_ref[p]
        l0 = _dot_nt(qp, ka)
        l1 = _dot_nt(qp, kb)
        c0 = N_HEADS + 2 * p
        score = score + jnp.maximum(l0, 0.0) * w[:, c0:c0 + 1] + jnp.maximum(l1, 0.0) * w[:, c0 + 1:c0 + 2]
    score = score + 0.0
    col = lax.broadcasted_iota(I32, (tq, ext), 1)
    tpos = lax.broadcasted_iota(I32, (tq, 1), 0) + i * tq
    limit = ((tpos >> 6) + 1) << 6
    valid = col < limit
    bits = lax.bitcast_convert_type(score, I32)
    key = jnp.where(bits < 0, bits ^ 0x7FFFFFFF, bits)
    key = jnp.where(valid, key, INT_MIN)

    def thr_body(it, tu):
        cand = tu | lax.shift_left(jnp.int32(1), 31 - it)
        cnt = _count(key >= (cand ^ INT_MIN))
        return jnp.where(cnt >= k_sel, cand, tu)

    tu = lax.fori_loop(0, 32, thr_body, jnp.zeros((tq, 1), I32))
    thr = tu ^ INT_MIN
    gt = key > thr
    ties = (key == thr) & valid
    n_gt = _count(gt)
    need = k_sel - n_gt
    excess = jnp.max(n_gt + _count(ties)) > k_sel

    nbits = max(1, int(np.ceil(np.log2(ext))))

    def tie_search():
        def tie_body(it, j):
            cand = j | lax.shift_left(jnp.int32(1), nbits - 1 - it)
            cnt = _count(ties & (col < cand))
            return jnp.where(cnt < need, cand, j)

        return lax.fori_loop(0, nbits, tie_body, jnp.zeros((tq, 1), I32))

    j = lax.cond(excess, tie_search, lambda: jnp.full((tq, 1), ext, I32))
    sel = gt | (ties & (col <= j))
    maskb = jnp.where(sel, 0.0, NEG)

    kd, vd = kd_ref[0, :ext, :], vd_ref[0, :ext, :]
    for g in range(N_HEADS // DSA_HGROUP):
        q = qd_ref[g * DSA_HGROUP:(g + 1) * DSA_HGROUP].reshape(DSA_HGROUP * tq, HEAD_DIM)
        s = _dot_nt(q, kd) * scale + jnp.tile(maskb, (DSA_HGROUP, 1))
        m = jnp.max(s, axis=-1, keepdims=True)
        p = jnp.exp(s - m)
        l = jnp.sum(p, axis=-1, keepdims=True)
        o = _dot(p.astype(CDT), vd) / l
        for hh in range(DSA_HGROUP):
            h = g * DSA_HGROUP + hh
            o_ref[:, h * HEAD_DIM:(h + 1) * HEAD_DIM] = o[hh * tq:(hh + 1) * tq, :].astype(o_ref.dtype)


def _dsa_kernel(qd_ref, kd_ref, vd_ref, qi_ref, ka_ref, kb_ref, wi_ref, oin_ref, o_ref, *, tq, seq, kstep, k_sel, scale):
    del oin_ref
    i = pl.program_id(1)
    nsteps = ((i + 1) * tq - 1) // kstep + 1
    for n in range(1, seq // kstep + 1):
        @pl.when(nsteps == n)
        def _(n=n):
            _dsa_body(qd_ref, kd_ref, vd_ref, qi_ref, ka_ref, kb_ref, wi_ref, o_ref,
                      i=i, tq=tq, ext=n * kstep, k_sel=k_sel, scale=scale)


def dsa_call(z2, z3, tail, o_all, batch, seq):
    tq = _tile(seq, DSA_TQ)
    nq = seq // tq
    kstep = _tile(seq, DSA_KSTEP)
    k_sel = min(TOPK_MAX, seq // 4)
    return pl.pallas_call(
        functools.partial(_dsa_kernel, tq=tq, seq=seq, kstep=kstep, k_sel=k_sel, scale=HEAD_DIM ** -0.5),
        grid=(batch, nq),
        in_specs=[pl.BlockSpec((HB, tq, LANE), lambda b, i: (0, b * nq + i, 0)),
                  pl.BlockSpec((1, seq, LANE), lambda b, i: (HB, b, 0)),
                  pl.BlockSpec((1, seq, LANE), lambda b, i: (HB + 1, b, 0)),
                  pl.BlockSpec((HB, tq, LANE), lambda b, i: (0, b * nq + i, 0)),
                  pl.BlockSpec((1, seq, LANE), lambda b, i: (HB, b, 0)),
                  pl.BlockSpec((1, seq, LANE), lambda b, i: (HB + 1, b, 0)),
                  pl.BlockSpec((tq, LANE), lambda b, i: (b * nq + i, 0)),
                  pl.BlockSpec(memory_space=pl.ANY)],
        out_specs=pl.BlockSpec((None, tq, HB * LANE), lambda b, i: (3, b * nq + i, 0)),
        out_shape=jax.ShapeDtypeStruct(o_all.shape, o_all.dtype),
        input_output_aliases={7: 0},
        compiler_params=_cparams("parallel", "parallel"),
        name="attn_dsa",
    )(z2, z2, z2, z3, z3, z3, tail, o_all)


MERGE_CHUNK = 256


def _merge_kernel(u_ref, o_ref, wg_ref, bg_ref, wb_ref, out_ref, acc_ref, *, nbranch):
    n = pl.program_id(2)
    @pl.when(n == 0)
    def _():
        acc_ref[...] = jnp.zeros(acc_ref.shape, acc_ref.dtype)

    u, o = u_ref[...], o_ref[...]
    tn = out_ref.shape[1]
    cw = min(MERGE_CHUNK, tn)
    for c in range(0, tn, cw):
        g = _dot(u, wg_ref[0, :, c:c + cw]) + bg_ref[0, :, c:c + cw]
        acc_ref[:, c:c + cw] += _sigmoid(g) * _dot(o, wb_ref[0, :, c:c + cw])

    @pl.when(n == nbranch - 1)
    def _():
        out_ref[...] = acc_ref[...].astype(out_ref.dtype)


def merge_call(u, o_all, wg, bg, wb, layer):
    t, d = u.shape
    nbranch, bw = wb.shape[1], wb.shape[2]
    tm, tn = _tile(t, 1024), _tile(d, 1024)
    return pl.pallas_call(
        functools.partial(_merge_kernel, nbranch=nbranch),
        grid=(t // tm, d // tn, nbranch),
        in_specs=[pl.BlockSpec((tm, d), lambda i, j, n: (i, 0)),
                  pl.BlockSpec((None, tm, bw), lambda i, j, n: (n, i, 0)),
                  pl.BlockSpec((None, 1, d, tn), lambda i, j, n: (layer, n, 0, j)),
                  pl.BlockSpec((1, 1, tn), lambda i, j, n: (n, 0, j)),
                  pl.BlockSpec((None, 1, bw, tn), lambda i, j, n: (layer, n, 0, j))],
        out_specs=pl.BlockSpec((tm, tn), lambda i, j, n: (i, j)),
        out_shape=jax.ShapeDtypeStruct((t, d), CDT),
        scratch_shapes=[pltpu.VMEM((tm, tn), F32)],
        compiler_params=_cparams("parallel", "parallel", "arbitrary"),
        name="gated_merge",
    )(u, o_all, wg, bg, wb)


def _pack_w_in(w_in):
    off = np.concatenate([[0], np.cumsum(SPLIT_SIZES)])
    col = {n: w_in[:, off[k]:off[k + 1]] for k, n in enumerate(SPLIT_NAMES)}
    d = w_in.shape[0]
    z64 = jnp.zeros((d, 64), w_in.dtype)
    w1 = jnp.concatenate([col[n] for n in ("qa", "ka", "va", "qc", "kc", "vc")], axis=1)
    w2 = jnp.concatenate([col["qd"], col["kd"], col["vd"]], axis=1)
    w3 = jnp.concatenate([col["qi"], col["ki"], z64, z64, col["ki"], col["kr"], z64], axis=1)
    w4 = jnp.concatenate([col["cq"], col["ckv"]], axis=1)
    w5 = jnp.concatenate([col["fa"], col["wi"], jnp.zeros((d, LANE - 24), w_in.dtype)], axis=1)
    w5t = jnp.concatenate([w5[:, :24], jnp.zeros((d, TAIL_ROWS - 24), w_in.dtype)], axis=1).T
    return [w.astype(CDT) for w in (w1, w2, w3, w4, w5, w5t)]


def _pack_w_uq(w_uq):
    r = w_uq.shape[0]
    w = w_uq.reshape(r, N_HEADS, MLA_NOPE + MLA_ROPE)
    w = jnp.concatenate([w, jnp.zeros((r, N_HEADS, 2 * LANE - MLA_NOPE - MLA_ROPE), w_uq.dtype)], axis=-1)
    return w.reshape(r, N_HEADS * 2 * LANE).astype(CDT)


def kernel(x, p, w_in, b_f, g_cq, g_ckv, w_uq, w_ukv, rel_bias, w_branch, w_gate, b_gate, w_out, w1_gate, w1_up, w1_down, w2_gate, w2_up, w2_down, g_ffn1_pre, g_ffn1_post, g_mix_pre, g_mix_post, g_ffn2_pre, g_ffn2_post, g_ple_pre, g_ple_post, w_ple, w_ple_gate):
    batch, seq, d = x.shape
    depth = w_in.shape[0]
    t = batch * seq
    cos128, sin128 = _rope_tables(seq, HEAD_DIM // 2)
    cos64, sin64 = _rope_tables(seq, IDX_DIM // 2)

    w1_gate, w1_up, w1_down, w2_gate, w2_up, w2_down, w_gate, w_branch, w_out, w_ple_gate = (
        w.astype(CDT) for w in (w1_gate, w1_up, w1_down, w2_gate, w2_up, w2_down, w_gate, w_branch, w_out, w_ple_gate))

    h = x.reshape(t, d)
    u = rmsnorm_call(h, g_ffn1_pre[0])
    for i in range(depth):
        a = swiglu_call(u, w1_gate, w1_up, i)
        h, u = rowres_call(a, w1_down, i, h, g_ffn1_post[i], g_mix_pre[i], 0.5)

        w1, w2, w3, w4, w5, w5t = _pack_w_in(w_in[i])
        z1 = mm_call(u, w1)
        z2 = proj_rope_call(u, w2, cos128, sin128, seq, HEAD_DIM // 2, range(HB + 1))
        z3 = proj_rope_call(u, w3, cos64, sin64, seq, IDX_DIM // 2, range(HB + 3))
        qm, kvb = proj_mla_call(u, w4, g_cq[i], g_ckv[i], _pack_w_uq(w_uq[i]), w_ukv[i].astype(CDT), cos64, sin64, seq)
        tail, tailt = proj_tail_call(u, w5, w5t, batch, seq)
        drow = decay_call(tailt, b_f[i], batch, seq)
        o_all = fox_call(z1, drow, batch, seq)
        o_all = mla_call(qm, kvb, z3, HB + 2, o_all, batch, seq)
        o_all = band_call(z1, band_bias_call(rel_bias[i]), o_all, batch, seq)
        o_all = dsa_call(z2, z3, tail, o_all, batch, seq)
        merged = merge_call(u, o_all, w_gate, b_gate[i].reshape(-1, 1, d), w_branch, i)
        h, u = rowres_call(merged, w_out, i, h, g_mix_post[i], g_ffn2_pre[i], 1.0)

        a = swiglu_call(u, w2_gate, w2_up, i)
        h, u = rowres_call(a, w2_down, i, h, g_ffn2_post[i], g_ple_pre[i], 0.5)

        last = i == depth - 1
        g_next = g_ple_pre[i] if last else g_ffn1_pre[i + 1]
        h, u = rowres_call(u, w_ple_gate, i, h, g_ple_post[i], g_next, 1.0,
                           ple=(p[i].reshape(t, -1), w_ple[i].astype(CDT)), emit_u=not last)
    return h.reshape(batch, seq, d)
```

```python
import functools

import numpy as np
import jax
import jax.numpy as jnp
from jax import lax
from jax.experimental import pallas as pl
from jax.experimental.pallas import tpu as pltpu

F32 = jnp.float32
I32 = jnp.int32
CDT = jnp.bfloat16
EPS = 1e-6
NEG = -1e30
ROPE_THETA = 10000.0

LANE = 128
HEAD_DIM = 128
N_HEADS = 8
HB = N_HEADS * HEAD_DIM // LANE
N_MIXERS = 4
CHUNK = 64
MLA_Q_LORA, MLA_KV_LORA, MLA_NOPE, MLA_ROPE = 768, 512, 128, 64
CH_PREV, MAX_REL = 8, 128
IDX_HEADS, IDX_DIM = 16, 64
TOPK_MAX = 256
SPLIT_SIZES = (1024, 1024, 1024, 8, 768, 512, 64, 1024, 1024, 1024, 1024, 128, 128, 1024, 64, 16)
SPLIT_NAMES = ("qa", "ka", "va", "fa", "cq", "ckv", "kr", "qc", "kc", "vc", "qd", "kd", "vd", "qi", "ki", "wi")
INT_MIN = -2147483648

VMEM_LIMIT_BYTES = 62 * 1024 * 1024


def _cparams(*sem):
    return pltpu.CompilerParams(dimension_semantics=sem, vmem_limit_bytes=VMEM_LIMIT_BYTES)


def _tile(n, pref):
    t = min(n, pref)
    assert n % t == 0, (n, pref)
    return t


def _dot(a, b):
    return jnp.dot(a, b, preferred_element_type=F32)


def _dot_nt(a, b):
    return lax.dot_general(a, b, (((1,), (1,)), ((), ())), preferred_element_type=F32)


def _rms(y, g):
    ms = jnp.mean(y * y, axis=-1, keepdims=True)
    return y * lax.rsqrt(ms + EPS) * g


def _sigmoid(x):
    return 1.0 / (1.0 + jnp.exp(-x))


def _rmsnorm_kernel(x_ref, g_ref, o_ref):
    o_ref[...] = _rms(x_ref[...], g_ref[...]).astype(o_ref.dtype)


def rmsnorm_call(x, g):
    t, d = x.shape
    tm = _tile(t, 256)
    return pl.pallas_call(
        _rmsnorm_kernel,
        grid=(t // tm,),
        in_specs=[pl.BlockSpec((tm, d), lambda i: (i, 0)), pl.BlockSpec((1, d), lambda i: (0, 0))],
        out_specs=pl.BlockSpec((tm, d), lambda i: (i, 0)),
        out_shape=jax.ShapeDtypeStruct((t, d), CDT),
        compiler_params=_cparams("parallel"),
        name="rmsnorm",
    )(x, g.reshape(1, d))


def _swiglu_kernel(a_ref, wg_ref, wu_ref, o_ref):
    a = a_ref[...]
    g = _dot(a, wg_ref[...])
    v = _dot(a, wu_ref[...])
    o_ref[...] = (g * _sigmoid(g) * v).astype(o_ref.dtype)


def swiglu_call(u, wg, wu, layer):
    t, d = u.shape
    f = wg.shape[2]
    tm, tn = _tile(t, 1024), _tile(f, 512)
    return pl.pallas_call(
        _swiglu_kernel,
        grid=(t // tm, f // tn),
        in_specs=[pl.BlockSpec((tm, d), lambda i, j: (i, 0)),
                  pl.BlockSpec((None, d, tn), lambda i, j: (layer, 0, j)),
                  pl.BlockSpec((None, d, tn), lambda i, j: (layer, 0, j))],
        out_specs=pl.BlockSpec((tm, tn), lambda i, j: (i, j)),
        out_shape=jax.ShapeDtypeStruct((t, f), CDT),
        compiler_params=_cparams("parallel", "parallel"),
        name="swiglu_up",
    )(u, wg, wu)


ROW_CHUNK = 64
ROWRES_TK = 4096
ROWRES_TN = 1024
STREAM_PARTS = 4


def _rowres_unpack(refs, ple, emit_u):
    a_ref, w_ref, h_ref, gp_ref, gn_ref = refs[:5]
    pos = 5
    p_ref = wple_ref = None
    if ple:
        p_ref, wple_ref = refs[pos:pos + 2]
        pos += 2
    hout_ref = refs[pos]
    uout_ref = refs[pos + 1] if emit_u else None
    return a_ref, w_ref, h_ref, gp_ref, gn_ref, p_ref, wple_ref, hout_ref, uout_ref


def _rowres_epilogue(h_ref, gp_ref, gn_ref, p_ref, wple_ref, hout_ref, uout_ref, coef, row0=None, nrows=None):
    tm = hout_ref.shape[0] if nrows is None else nrows
    rc = min(ROW_CHUNK, tm)

    def chunk(rows):
        y = hout_ref[rows, :]
        if p_ref is not None:
            e = _dot(p_ref[rows, :].astype(CDT), wple_ref[...])
            y = _sigmoid(y) * e
        hn = h_ref[rows, :] + coef * _rms(y, gp_ref[...])
        hout_ref[rows, :] = hn
        if uout_ref is not None:
            uout_ref[rows, :] = _rms(hn, gn_ref[...]).astype(uout_ref.dtype)

    if row0 is not None:
        for r in range(row0, row0 + nrows, rc):
            chunk(slice(r, r + rc))
        return

    def body(r, carry):
        chunk(pl.ds(pl.multiple_of(r * rc, rc), rc))
        return carry

    lax.fori_loop(0, tm // rc, body, 0)


def _rowres_kernel(*refs, coef, nk, nj, tn, ple, emit_u):
    a_ref, w_ref, h_hbm, gp_ref, gn_ref, p_ref, wple_ref, hout_ref, uout_ref = _rowres_unpack(refs[:-2], ple, emit_u)
    h_ref, h_sem = refs[-2:]
    i, k, j = pl.program_id(0), pl.program_id(1), pl.program_id(2)
    tm = h_ref.shape[0]
    h_copy = pltpu.make_async_copy(h_hbm.at[pl.ds(pl.multiple_of(i * tm, tm), tm), :], h_ref, h_sem)

    @pl.when((k == 0) & (j == 0))
    def _():
        h_copy.start()

    last = (k == nk - 1) & (j == nj - 1)

    @pl.when(jnp.logical_not(last))
    def _():
        cols = pl.ds(pl.multiple_of(j * tn, tn), tn)
        part = _dot(a_ref[...], w_ref[...])

        @pl.when(k == 0)
        def _():
            hout_ref[:, cols] = part

        @pl.when(k > 0)
        def _():
            hout_ref[:, cols] += part

    @pl.when(last)
    def _():
        h_copy.wait()
        c0 = (nj - 1) * tn
        rp = tm // STREAM_PARTS
        for row0 in range(0, tm, rp):
            rows = slice(row0, row0 + rp)
            part = _dot(a_ref[rows, :], w_ref[...])
            if nk == 1:
                hout_ref[rows, c0:c0 + tn] = part
            else:
                hout_ref[rows, c0:c0 + tn] += part
            _rowres_epilogue(h_ref, gp_ref, gn_ref, p_ref, wple_ref, hout_ref, uout_ref, coef, row0, rp)


def _rowres_resident_kernel(*refs, coef, ple, emit_u):
    a_ref, w_ref, h_ref, gp_ref, gn_ref, p_ref, wple_ref, hout_ref, uout_ref = _rowres_unpack(refs, ple, emit_u)
    tm, d = hout_ref.shape
    tn = min(RESIDENT_TN, d)
    part = tm // RESIDENT_PARTS
    for row0 in range(0, tm, part):
        rows = slice(row0, row0 + part)
        a = a_ref[rows, :]
        if ple:
            pe = p_ref[rows, :].astype(CDT)
        for c in range(0, d, tn):
            y = _dot(a, w_ref[:, c:c + tn])
            if ple:
                y = _sigmoid(y) * _dot(pe, wple_ref[:, c:c + tn])
            hout_ref[rows, c:c + tn] = y
        _rowres_epilogue(h_ref, gp_ref, gn_ref, None, None, hout_ref, uout_ref, coef, row0, part)


RESIDENT_W_BYTES = 32 * 1024 * 1024
RESIDENT_TM = 256
RESIDENT_PARTS = 2
RESIDENT_TN = 512


def _rowres_resident_call(a, w, layer, h, g_post, g_next, coef, ple, emit_u):
    t, kdim = a.shape
    d = w.shape[2]
    tm = _tile(t, RESIDENT_TM)
    once = pl.Buffered(1)
    in_specs = [pl.BlockSpec((tm, kdim), lambda i: (i, 0)),
                pl.BlockSpec((None, kdim, d), lambda i: (layer, 0, 0), pipeline_mode=once),
                pl.BlockSpec((tm, d), lambda i: (i, 0)),
                pl.BlockSpec((1, d), lambda i: (0, 0), pipeline_mode=once),
                pl.BlockSpec((1, d), lambda i: (0, 0), pipeline_mode=once)]
    args = [a, w, h, g_post.reshape(1, d), g_next.reshape(1, d)]
    if ple is not None:
        p, wple = ple
        in_specs += [pl.BlockSpec((tm, p.shape[1]), lambda i: (i, 0)),
                     pl.BlockSpec(wple.shape, lambda i: (0, 0), pipeline_mode=once)]
        args += [p, wple]
    out_specs = [pl.BlockSpec((tm, d), lambda i: (i, 0))]
    out_shape = [jax.ShapeDtypeStruct((t, d), F32)]
    if emit_u:
        out_specs.append(pl.BlockSpec((tm, d), lambda i: (i, 0)))
        out_shape.append(jax.ShapeDtypeStruct((t, d), CDT))
    res = pl.pallas_call(
        functools.partial(_rowres_resident_kernel, coef=coef, ple=ple is not None, emit_u=emit_u),
        grid=(t // tm,),
        in_specs=in_specs,
        out_specs=out_specs,
        out_shape=out_shape,
        compiler_params=_cparams("arbitrary"),
        name="rowres_ple" if ple is not None else "rowres_res",
    )(*args)
    return (res[0], res[1]) if emit_u else (res[0], None)


def rowres_call(a, w, layer, h, g_post, g_next, coef, ple=None, emit_u=True):
    t, kdim = a.shape
    d = w.shape[2]
    if kdim * d * jnp.dtype(w.dtype).itemsize <= RESIDENT_W_BYTES:
        return _rowres_resident_call(a, w, layer, h, g_post, g_next, coef, ple, emit_u)
    tm, tk, tn = _tile(t, 512), _tile(kdim, ROWRES_TK), _tile(d, ROWRES_TN)
    nk, nj = kdim // tk, d // tn
    once = pl.Buffered(1)
    in_specs = [pl.BlockSpec((tm, tk), lambda i, k, j: (i, k)),
                pl.BlockSpec((None, tk, tn), lambda i, k, j: (layer, k, j)),
                pl.BlockSpec(memory_space=pl.ANY),
                pl.BlockSpec((1, d), lambda i, k, j: (0, 0), pipeline_mode=once),
                pl.BlockSpec((1, d), lambda i, k, j: (0, 0), pipeline_mode=once)]
    args = [a, w, h, g_post.reshape(1, d), g_next.reshape(1, d)]
    if ple is not None:
        p, wple = ple
        in_specs += [pl.BlockSpec((tm, p.shape[1]), lambda i, k, j: (i, 0), pipeline_mode=once),
                     pl.BlockSpec(wple.shape, lambda i, k, j: (0, 0), pipeline_mode=once)]
        args += [p, wple]
    out_specs = [pl.BlockSpec((tm, d), lambda i, k, j: (i, 0))]
    out_shape = [jax.ShapeDtypeStruct((t, d), F32)]
    if emit_u:
        out_specs.append(pl.BlockSpec((tm, d), lambda i, k, j: (i, 0)))
        out_shape.append(jax.ShapeDtypeStruct((t, d), CDT))
    res = pl.pallas_call(
        functools.partial(_rowres_kernel, coef=coef, nk=nk, nj=nj, tn=tn, ple=ple is not None, emit_u=emit_u),
        grid=(t // tm, nk, nj),
        in_specs=in_specs,
        out_specs=out_specs,
        out_shape=out_shape,
        scratch_shapes=[pltpu.VMEM((tm, d), F32), pltpu.SemaphoreType.DMA(())],
        compiler_params=_cparams("arbitrary", "arbitrary", "arbitrary"),
        name="rowres_ple" if ple is not None else "rowres",
    )(*args)
    return (res[0], res[1]) if emit_u else (res[0], None)


def _mm_kernel(a_ref, w_ref, o_ref):
    acc = _dot(a_ref[...], w_ref[...])
    for blk in range(o_ref.shape[0]):
        o_ref[blk] = acc[:, blk * LANE:(blk + 1) * LANE].astype(o_ref.dtype)


def mm_call(a, w):
    t, d = a.shape
    n = w.shape[1]
    tm, tn = _tile(t, 1024), _tile(n, 1024)
    return pl.pallas_call(
        _mm_kernel,
        grid=(t // tm, n // tn),
        in_specs=[pl.BlockSpec((tm, d), lambda i, j: (i, 0)), pl.BlockSpec((d, tn), lambda i, j: (0, j))],
        out_specs=pl.BlockSpec((tn // LANE, tm, LANE), lambda i, j: (j, i, 0)),
        out_shape=jax.ShapeDtypeStruct((n // LANE, t, LANE), CDT),
        compiler_params=_cparams("parallel", "parallel"),
        name="proj_plain",
    )(a, w)


def _rope_tables(seq, half):
    inv = ROPE_THETA ** (-jnp.arange(half, dtype=F32) / half)
    ang = jnp.arange(seq, dtype=F32)[:, None] * inv[None, :]
    cos, sin = jnp.cos(ang), jnp.sin(ang)
    reps = LANE // (2 * half)
    return jnp.tile(jnp.concatenate([cos, cos], -1), (1, reps)), jnp.tile(jnp.concatenate([-sin, sin], -1), (1, reps))


def _rope_block(x, c, s, half):
    if 2 * half == LANE:
        sw = pltpu.roll(x, half, axis=1)
    else:
        lane = lax.broadcasted_iota(I32, x.shape, 1)
        first = (lane & (2 * half - 1)) < half
        sw = jnp.where(first, pltpu.roll(x, LANE - half, axis=1), pltpu.roll(x, half, axis=1))
    return x * c + sw * s


def _proj_rope_kernel(a_ref, w_ref, c_ref, s_ref, o_ref, *, half, rope_blocks):
    acc = _dot(a_ref[...], w_ref[...])
    c, s = c_ref[...], s_ref[...]
    for blk in range(o_ref.shape[0]):
        x = acc[:, blk * LANE:(blk + 1) * LANE]
        if blk in rope_blocks:
            x = _rope_block(x, c, s, half)
        o_ref[blk] = x.astype(o_ref.dtype)


def proj_rope_call(u, w, cos, sin, seq, half, rope_blocks):
    t, d = u.shape
    n = w.shape[1]
    tm = _tile(seq, 512)
    ns = seq // tm
    return pl.pallas_call(
        functools.partial(_proj_rope_kernel, half=half, rope_blocks=tuple(rope_blocks)),
        grid=(t // tm,),
        in_specs=[pl.BlockSpec((tm, d), lambda i: (i, 0)),
                  pl.BlockSpec((d, n), lambda i: (0, 0)),
                  pl.BlockSpec((tm, LANE), lambda i: (i % ns, 0)),
                  pl.BlockSpec((tm, LANE), lambda i: (i % ns, 0))],
        out_specs=pl.BlockSpec((n // LANE, tm, LANE), lambda i: (0, i, 0)),
        out_shape=jax.ShapeDtypeStruct((n // LANE, t, LANE), CDT),
        compiler_params=_cparams("parallel"),
        name=f"proj_rope{2 * half}",
    )(u, w, cos, sin)


def _proj_mla_kernel(a_ref, w_ref, gq_ref, gkv_ref, wuq_ref, wukv_ref, c_ref, s_ref, q_ref, kv_ref):
    acc = _dot(a_ref[...], w_ref[...])
    cq = _rms(acc[:, :MLA_Q_LORA], gq_ref[...]).astype(CDT)
    ckv = _rms(acc[:, MLA_Q_LORA:MLA_Q_LORA + MLA_KV_LORA], gkv_ref[...]).astype(CDT)
    qb = _dot(cq, wuq_ref[...])
    c, s = c_ref[...], s_ref[...]
    for blk in range(q_ref.shape[0]):
        x = qb[:, blk * LANE:(blk + 1) * LANE]
        if blk % 2 == 1:
            x = _rope_block(x, c, s, MLA_ROPE // 2)
        q_ref[blk] = x.astype(q_ref.dtype)
    kvb = _dot(ckv, wukv_ref[...])
    for blk in range(kv_ref.shape[0]):
        kv_ref[blk] = kvb[:, blk * LANE:(blk + 1) * LANE].astype(kv_ref.dtype)


def proj_mla_call(u, w, gq, gkv, wuq, wukv, cos, sin, seq):
    t, d = u.shape
    n = w.shape[1]
    tm = _tile(seq, 512)
    ns = seq // tm
    nq, nkv = wuq.shape[1] // LANE, wukv.shape[1] // LANE
    return pl.pallas_call(
        _proj_mla_kernel,
        grid=(t // tm,),
        in_specs=[pl.BlockSpec((tm, d), lambda i: (i, 0)),
                  pl.BlockSpec((d, n), lambda i: (0, 0)),
                  pl.BlockSpec((1, MLA_Q_LORA), lambda i: (0, 0)),
                  pl.BlockSpec((1, MLA_KV_LORA), lambda i: (0, 0)),
                  pl.BlockSpec(wuq.shape, lambda i: (0, 0)),
                  pl.BlockSpec(wukv.shape, lambda i: (0, 0)),
                  pl.BlockSpec((tm, LANE), lambda i: (i % ns, 0)),
                  pl.BlockSpec((tm, LANE), lambda i: (i % ns, 0))],
        out_specs=[pl.BlockSpec((nq, tm, LANE), lambda i: (0, i, 0)), pl.BlockSpec((nkv, tm, LANE), lambda i: (0, i, 0))],
        out_shape=[jax.ShapeDtypeStruct((nq, t, LANE), CDT), jax.ShapeDtypeStruct((nkv, t, LANE), CDT)],
        compiler_params=_cparams("parallel"),
        name="proj_mla",
    )(u, w, gq.reshape(1, -1), gkv.reshape(1, -1), wuq, wukv, cos, sin)


TAIL_ROWS = 32


def _proj_tail_kernel(a_ref, w_ref, wt_ref, o_ref, ot_ref):
    a = a_ref[...]
    o_ref[...] = _dot(a, w_ref[...])
    ot_ref[0] = _dot_nt(wt_ref[...], a)


def proj_tail_call(u, w, wt, batch, seq):
    t, d = u.shape
    tm = _tile(seq, 512)
    ns = seq // tm
    return pl.pallas_call(
        _proj_tail_kernel,
        grid=(t // tm,),
        in_specs=[pl.BlockSpec((tm, d), lambda i: (i, 0)),
                  pl.BlockSpec((d, LANE), lambda i: (0, 0)),
                  pl.BlockSpec((TAIL_ROWS, d), lambda i: (0, 0))],
        out_specs=[pl.BlockSpec((tm, LANE), lambda i: (i, 0)),
                   pl.BlockSpec((1, TAIL_ROWS, tm), lambda i: (i // ns, 0, i % ns))],
        out_shape=[jax.ShapeDtypeStruct((t, LANE), F32), jax.ShapeDtypeStruct((batch, TAIL_ROWS, seq), F32)],
        compiler_params=_cparams("parallel"),
        name="proj_tail",
    )(u, w, wt)


def _log_sigmoid(x):
    return jnp.minimum(x, 0.0) - jnp.log1p(jnp.exp(-jnp.abs(x)))


def _decay_kernel(tailt_ref, brow_ref, drow_ref):
    y = _log_sigmoid(tailt_ref[0, :N_HEADS, :] + brow_ref[...])
    seq = y.shape[1]
    idy = lax.broadcasted_iota(I32, y.shape, 1)
    d = 1
    while d < seq:
        y = y + jnp.where(idy >= d, pltpu.roll(y, d, axis=1), 0.0)
        d *= 2
    drow_ref[0] = y


def decay_call(tailt, b_f, batch, seq):
    return pl.pallas_call(
        _decay_kernel,
        grid=(batch,),
        in_specs=[pl.BlockSpec((1, TAIL_ROWS, seq), lambda b: (b, 0, 0)),
                  pl.BlockSpec((N_HEADS, 1), lambda b: (0, 0))],
        out_specs=pl.BlockSpec((1, N_HEADS, seq), lambda b: (b, 0, 0)),
        out_shape=jax.ShapeDtypeStruct((batch, N_HEADS, seq), F32),
        compiler_params=_cparams("parallel"),
        name="decay_scan",
    )(tailt, b_f.reshape(N_HEADS, 1))


ATT_TQ = 512
ATT_SUB = 2


def _prefix_attention(i, tq, nq, logits_fn, v_fn, allowed_fn, o_ref):
    sq = tq // ATT_SUB
    row = lax.broadcasted_iota(I32, (sq, sq), 0)
    col = lax.broadcasted_iota(I32, (sq, sq), 1)
    for n in range(nq):
        @pl.when(i == n)
        def _(n=n):
            for r in range(ATT_SUB):
                pre = n * tq + r * sq
                sd = jnp.where(allowed_fn(row, col), logits_fn(r, pre, sq), NEG)
                m = jnp.max(sd, axis=-1, keepdims=True)
                if pre:
                    sp = logits_fn(r, 0, pre)
                    m = jnp.maximum(m, jnp.max(sp, axis=-1, keepdims=True))
                pd = jnp.exp(sd - m)
                l = jnp.sum(pd, axis=-1, keepdims=True)
                acc = _dot(pd.astype(CDT), v_fn(pre, sq))
                if pre:
                    pp = jnp.exp(sp - m)
                    l = l + jnp.sum(pp, axis=-1, keepdims=True)
                    acc = acc + _dot(pp.astype(CDT), v_fn(0, pre))
                o_ref[r * sq:(r + 1) * sq, :] = (acc / l).astype(o_ref.dtype)


def _fox_kernel(q_ref, k_ref, v_ref, drow_ref, o_ref, *, tq, nq, scale):
    sq = tq // ATT_SUB
    qs = [q_ref[0, r * sq:(r + 1) * sq, :] for r in range(ATT_SUB)]

    def logits(r, start, n):
        return _dot_nt(qs[r], k_ref[0, start:start + n, :]) * scale - drow_ref[0, :, start:start + n]

    o_ref[1:] = jnp.zeros((N_MIXERS - 1,) + o_ref.shape[1:], o_ref.dtype)
    _prefix_attention(pl.program_id(2), tq, nq, logits, lambda start, n: v_ref[0, start:start + n, :],
                      lambda row, col: col <= row, o_ref.at[0])


def _mla_kernel(qn_ref, qr_ref, kn_ref, kr_ref, v_ref, oin_ref, o_ref, *, tq, nq, scale):
    del oin_ref
    sq = tq // ATT_SUB
    qs = [jnp.concatenate([qn_ref[0, r * sq:(r + 1) * sq, :], qr_ref[0, r * sq:(r + 1) * sq, :]], axis=1)
          for r in range(ATT_SUB)]

    def logits(r, start, n):
        k = jnp.concatenate([kn_ref[0, start:start + n, :], kr_ref[0, start:start + n, :]], axis=1)
        return _dot_nt(qs[r], k) * scale

    _prefix_attention(pl.program_id(2), tq, nq, logits, lambda start, n: v_ref[0, start:start + n, :],
                      lambda row, col: (col >> 6) <= (row >> 6), o_ref)


def fox_call(z1, drow, batch, seq):
    tq = _tile(seq, ATT_TQ)
    nq = seq // tq
    drow = drow.reshape(batch * N_HEADS, 1, seq)
    t = batch * seq
    return pl.pallas_call(
        functools.partial(_fox_kernel, tq=tq, nq=nq, scale=HEAD_DIM ** -0.5),
        grid=(batch, N_HEADS, nq),
        in_specs=[pl.BlockSpec((1, tq, LANE), lambda b, h, i: (h, b * nq + i, 0)),
                  pl.BlockSpec((1, seq, LANE), lambda b, h, i: (HB + h, b, 0)),
                  pl.BlockSpec((1, seq, LANE), lambda b, h, i: (2 * HB + h, b, 0)),
                  pl.BlockSpec((1, 1, seq), lambda b, h, i: (b * N_HEADS + h, 0, 0))],
        out_specs=pl.BlockSpec((N_MIXERS, tq, LANE), lambda b, h, i: (0, b * nq + i, h)),
        out_shape=jax.ShapeDtypeStruct((N_MIXERS, t, HB * LANE), CDT),
        compiler_params=_cparams("parallel", "parallel", "parallel"),
        name="attn_fox",
    )(z1, z1, z1, drow)


def mla_call(qm, kvb, z3, kr_block, o_all, batch, seq):
    tq = _tile(seq, ATT_TQ)
    nq = seq // tq
    return pl.pallas_call(
        functools.partial(_mla_kernel, tq=tq, nq=nq, scale=(MLA_NOPE + MLA_ROPE) ** -0.5),
        grid=(batch, N_HEADS, nq),
        in_specs=[pl.BlockSpec((1, tq, LANE), lambda b, h, i: (2 * h, b * nq + i, 0)),
                  pl.BlockSpec((1, tq, LANE), lambda b, h, i: (2 * h + 1, b * nq + i, 0)),
                  pl.BlockSpec((1, seq, LANE), lambda b, h, i: (2 * h, b, 0)),
                  pl.BlockSpec((1, seq, LANE), lambda b, h, i: (kr_block, b, 0)),
                  pl.BlockSpec((1, seq, LANE), lambda b, h, i: (2 * h + 1, b, 0)),
                  pl.BlockSpec(memory_space=pl.ANY)],
        out_specs=pl.BlockSpec((None, tq, LANE), lambda b, h, i: (1, b * nq + i, h)),
        out_shape=jax.ShapeDtypeStruct(o_all.shape, o_all.dtype),
        input_output_aliases={5: 0},
        compiler_params=_cparams("parallel", "parallel", "parallel"),
        name="attn_mla",
    )(qm, qm, kvb, z3, kvb, o_all)


BAND_TQ = 256
BAND_NKB = (CH_PREV * CHUNK) // BAND_TQ + 1
BIAS_STRIP = 128


def _band_bias_kernel(tab_ref, o_ref):
    h = pl.program_id(0)
    back = (BAND_NKB - 1) * BAND_TQ
    for kb in range(BAND_NKB):
        for q0 in range(0, BAND_TQ, BIAS_STRIP):
            for k0 in range(0, BAND_TQ, BIAS_STRIP):
                q = lax.broadcasted_iota(I32, (BIAS_STRIP, BIAS_STRIP), 0) + q0
                koff = lax.broadcasted_iota(I32, (BIAS_STRIP, BIAS_STRIP), 1) + (kb * BAND_TQ + k0 - back)
                rel = jnp.clip(q - koff, -MAX_REL, MAX_REL) + MAX_REL
                koff0 = kb * BAND_TQ + k0 - back
                lo = int(np.clip(q0 - (koff0 + BIAS_STRIP - 1), -MAX_REL, MAX_REL)) + MAX_REL
                hi = int(np.clip(q0 + BIAS_STRIP - 1 - koff0, -MAX_REL, MAX_REL)) + MAX_REL

                def body(r, acc, rel=rel):
                    return jnp.where(rel == r, tab_ref[h, r], acc)

                acc = lax.fori_loop(lo, hi + 1, body, jnp.zeros((BIAS_STRIP, BIAS_STRIP), F32))
                kc, qc = koff >> 6, q >> 6
                band = (kc >= qc - CH_PREV) & (kc <= qc)
                o_ref[0, kb, q0:q0 + BIAS_STRIP, k0:k0 + BIAS_STRIP] = jnp.where(band, acc, NEG)


def band_bias_call(rel_table):
    return pl.pallas_call(
        _band_bias_kernel,
        grid=(N_HEADS,),
        in_specs=[pl.BlockSpec(memory_space=pltpu.SMEM)],
        out_specs=pl.BlockSpec((1, BAND_NKB, BAND_TQ, BAND_TQ), lambda h: (h, 0, 0, 0)),
        out_shape=jax.ShapeDtypeStruct((N_HEADS, BAND_NKB, BAND_TQ, BAND_TQ), F32),
        compiler_params=_cparams("parallel"),
        name="band_bias",
    )(rel_table.astype(F32))


def _band_kernel(q_ref, k_ref, v_ref, bias_ref, oin_ref, o_ref, *, tq, scale):
    del oin_ref
    i = pl.program_id(1)
    for h in range(N_HEADS):
        q = q_ref[h]
        ss, vs = [], []
        for kb in range(BAND_NKB):
            kt = i - (BAND_NKB - 1) + kb
            ks = pl.ds(pl.multiple_of(jnp.maximum(kt, 0) * tq, tq), tq)
            s = _dot_nt(q, k_ref[h, ks, :]) * scale + bias_ref[h, kb]
            if kb < BAND_NKB - 1:
                s = jnp.where(kt >= 0, s, NEG)
            ss.append(s)
            vs.append(v_ref[h, ks, :])
        m = ss[0].max(axis=-1, keepdims=True)
        for s in ss[1:]:
            m = jnp.maximum(m, s.max(axis=-1, keepdims=True))
        l = jnp.zeros((tq, 1), F32)
        acc = jnp.zeros((tq, HEAD_DIM), F32)
        for s, v in zip(ss, vs):
            p = jnp.exp(s - m)
            l = l + p.sum(axis=-1, keepdims=True)
            acc = acc + _dot(p.astype(CDT), v)
        o_ref[:, h * HEAD_DIM:(h + 1) * HEAD_DIM] = (acc / l).astype(o_ref.dtype)


def band_call(z1, bias_tiles, o_all, batch, seq):
    tq = BAND_TQ
    assert seq % tq == 0
    nq = seq // tq
    return pl.pallas_call(
        functools.partial(_band_kernel, tq=tq, scale=HEAD_DIM ** -0.5),
        grid=(batch, nq),
        in_specs=[pl.BlockSpec((HB, tq, LANE), lambda b, i: (3, b * nq + i, 0)),
                  pl.BlockSpec((HB, seq, LANE), lambda b, i: (4, b, 0)),
                  pl.BlockSpec((HB, seq, LANE), lambda b, i: (5, b, 0)),
                  pl.BlockSpec(bias_tiles.shape, lambda b, i: (0, 0, 0, 0)),
                  pl.BlockSpec(memory_space=pl.ANY)],
        out_specs=pl.BlockSpec((None, tq, HB * LANE), lambda b, i: (2, b * nq + i, 0)),
        out_shape=jax.ShapeDtypeStruct(o_all.shape, o_all.dtype),
        input_output_aliases={4: 0},
        compiler_params=_cparams("parallel", "parallel"),
        name="attn_band",
    )(z1, z1, z1, bias_tiles, o_all)


DSA_TQ = 256
DSA_KSTEP = 512
DSA_HGROUP = 2


def _count(mask):
    return jnp.sum(mask.astype(F32), axis=-1, keepdims=True)


def _dsa_body(qd_ref, kd_ref, vd_ref, qi_ref, ka_ref, kb_ref, wi_ref, o_ref, *, i, tq, ext, k_sel, scale):
    assert IDX_DIM == 64
    w = (wi_ref[...] * (IDX_HEADS ** -0.5)) * (IDX_DIM ** -0.5)
    ka, kb = ka_ref[0, :ext, :], kb_ref[0, :ext, :]
    score = jnp.zeros((tq, ext), F32)
    for p in range(IDX_HEADS // 2):
        qp = qi_ref[p]
        l0 = _dot_nt(qp, ka)
        l1 = _dot_nt(qp, kb)
        c0 = N_HEADS + 2 * p
        score = score + jnp.maximum(l0, 0.0) * w[:, c0:c0 + 1] + jnp.maximum(l1, 0.0) * w[:, c0 + 1:c0 + 2]
    score = score + 0.0
    col = lax.broadcasted_iota(I32, (tq, ext), 1)
    tpos = lax.broadcasted_iota(I32, (tq, 1), 0) + i * tq
    limit = ((tpos >> 6) + 1) << 6
    valid = col < limit
    bits = lax.bitcast_convert_type(score, I32)
    key = jnp.where(bits < 0, bits ^ 0x7FFFFFFF, bits)
    key = jnp.where(valid, key, INT_MIN)

    def thr_body(it, tu):
        cand = tu | lax.shift_left(jnp.int32(1), 31 - it)
        cnt = _count(key >= (cand ^ INT_MIN))
        return jnp.where(cnt >= k_sel, cand, tu)

    tu = lax.fori_loop(0, 32, thr_body, jnp.zeros((tq, 1), I32))
    thr = tu ^ INT_MIN
    gt = key > thr
    ties = (key == thr) & valid
    n_gt = _count(gt)
    need = k_sel - n_gt
    excess = jnp.max(n_gt + _count(ties)) > k_sel

    nbits = max(1, int(np.ceil(np.log2(ext))))

    def tie_search():
        def tie_body(it, j):
            cand = j | lax.shift_left(jnp.int32(1), nbits - 1 - it)
            cnt = _count(ties & (col < cand))
            return jnp.where(cnt < need, cand, j)

        return lax.fori_loop(0, nbits, tie_body, jnp.zeros((tq, 1), I32))

    j = lax.cond(excess, tie_search, lambda: jnp.full((tq, 1), ext, I32))
    sel = gt | (ties & (col <= j))
    maskb = jnp.where(sel, 0.0, NEG)

    kd, vd = kd_ref[0, :ext, :], vd_ref[0, :ext, :]
    for g in range(N_HEADS // DSA_HGROUP):
        q = qd_ref[g * DSA_HGROUP:(g + 1) * DSA_HGROUP].reshape(DSA_HGROUP * tq, HEAD_DIM)
        s = _dot_nt(q, kd) * scale + jnp.tile(maskb, (DSA_HGROUP, 1))
        m = jnp.max(s, axis=-1, keepdims=True)
        p = jnp.exp(s - m)
        l = jnp.sum(p, axis=-1, keepdims=True)
        o = _dot(p.astype(CDT), vd) / l
        for hh in range(DSA_HGROUP):
            h = g * DSA_HGROUP + hh
            o_ref[:, h * HEAD_DIM:(h + 1) * HEAD_DIM] = o[hh * tq:(hh + 1) * tq, :].astype(o_ref.dtype)


def _dsa_kernel(qd_ref, kd_ref, vd_ref, qi_ref, ka_ref, kb_ref, wi_ref, oin_ref, o_ref, *, tq, seq, kstep, k_sel, scale):
    del oin_ref
    i = pl.program_id(1)
    nsteps = ((i + 1) * tq - 1) // kstep + 1
    for n in range(1, seq // kstep + 1):
        @pl.when(nsteps == n)
        def _(n=n):
            _dsa_body(qd_ref, kd_ref, vd_ref, qi_ref, ka_ref, kb_ref, wi_ref, o_ref,
                      i=i, tq=tq, ext=n * kstep, k_sel=k_sel, scale=scale)


def dsa_call(z2, z3, tail, o_all, batch, seq):
    tq = _tile(seq, DSA_TQ)
    nq = seq // tq
    kstep = _tile(seq, DSA_KSTEP)
    k_sel = min(TOPK_MAX, seq // 4)
    return pl.pallas_call(
        functools.partial(_dsa_kernel, tq=tq, seq=seq, kstep=kstep, k_sel=k_sel, scale=HEAD_DIM ** -0.5),
        grid=(batch, nq),
        in_specs=[pl.BlockSpec((HB, tq, LANE), lambda b, i: (0, b * nq + i, 0)),
                  pl.BlockSpec((1, seq, LANE), lambda b, i: (HB, b, 0)),
                  pl.BlockSpec((1, seq, LANE), lambda b, i: (HB + 1, b, 0)),
                  pl.BlockSpec((HB, tq, LANE), lambda b, i: (0, b * nq + i, 0)),
                  pl.BlockSpec((1, seq, LANE), lambda b, i: (HB, b, 0)),
                  pl.BlockSpec((1, seq, LANE), lambda b, i: (HB + 1, b, 0)),
                  pl.BlockSpec((tq, LANE), lambda b, i: (b * nq + i, 0)),
                  pl.BlockSpec(memory_space=pl.ANY)],
        out_specs=pl.BlockSpec((None, tq, HB * LANE), lambda b, i: (3, b * nq + i, 0)),
        out_shape=jax.ShapeDtypeStruct(o_all.shape, o_all.dtype),
        input_output_aliases={7: 0},
        compiler_params=_cparams("parallel", "parallel"),
        name="attn_dsa",
    )(z2, z2, z2, z3, z3, z3, tail, o_all)


MERGE_CHUNK = 256


def _merge_kernel(u_ref, o_ref, wg_ref, bg_ref, wb_ref, out_ref, acc_ref, *, nbranch):
    n = pl.program_id(2)
    @pl.when(n == 0)
    def _():
        acc_ref[...] = jnp.zeros(acc_ref.shape, acc_ref.dtype)

    u, o = u_ref[...], o_ref[...]
    tn = out_ref.shape[1]
    cw = min(MERGE_CHUNK, tn)
    for c in range(0, tn, cw):
        g = _dot(u, wg_ref[0, :, c:c + cw]) + bg_ref[0, :, c:c + cw]
        acc_ref[:, c:c + cw] += _sigmoid(g) * _dot(o, wb_ref[0, :, c:c + cw])

    @pl.when(n == nbranch - 1)
    def _():
        out_ref[...] = acc_ref[...].astype(out_ref.dtype)


def merge_call(u, o_all, wg, bg, wb, layer):
    t, d = u.shape
    nbranch, bw = wb.shape[1], wb.shape[2]
    tm, tn = _tile(t, 1024), _tile(d, 1024)
    return pl.pallas_call(
        functools.partial(_merge_kernel, nbranch=nbranch),
        grid=(t // tm, d // tn, nbranch),
        in_specs=[pl.BlockSpec((tm, d), lambda i, j, n: (i, 0)),
                  pl.BlockSpec((None, tm, bw), lambda i, j, n: (n, i, 0)),
                  pl.BlockSpec((None, 1, d, tn), lambda i, j, n: (layer, n, 0, j)),
                  pl.BlockSpec((1, 1, tn), lambda i, j, n: (n, 0, j)),
                  pl.BlockSpec((None, 1, bw, tn), lambda i, j, n: (layer, n, 0, j))],
        out_specs=pl.BlockSpec((tm, tn), lambda i, j, n: (i, j)),
        out_shape=jax.ShapeDtypeStruct((t, d), CDT),
        scratch_shapes=[pltpu.VMEM((tm, tn), F32)],
        compiler_params=_cparams("parallel", "parallel", "arbitrary"),
        name="gated_merge",
    )(u, o_all, wg, bg, wb)


def _pack_w_in(w_in):
    off = np.concatenate([[0], np.cumsum(SPLIT_SIZES)])
    col = {n: w_in[:, off[k]:off[k + 1]] for k, n in enumerate(SPLIT_NAMES)}
    d = w_in.shape[0]
    z64 = jnp.zeros((d, 64), w_in.dtype)
    w1 = jnp.concatenate([col[n] for n in ("qa", "ka", "va", "qc", "kc", "vc")], axis=1)
    w2 = jnp.concatenate([col["qd"], col["kd"], col["vd"]], axis=1)
    w3 = jnp.concatenate([col["qi"], col["ki"], z64, z64, col["ki"], col["kr"], z64], axis=1)
    w4 = jnp.concatenate([col["cq"], col["ckv"]], axis=1)
    w5 = jnp.concatenate([col["fa"], col["wi"], jnp.zeros((d, LANE - 24), w_in.dtype)], axis=1)
    w5t = jnp.concatenate([w5[:, :24], jnp.zeros((d, TAIL_ROWS - 24), w_in.dtype)], axis=1).T
    return [w.astype(CDT) for w in (w1, w2, w3, w4, w5, w5t)]


def _pack_w_uq(w_uq):
    r = w_uq.shape[0]
    w = w_uq.reshape(r, N_HEADS, MLA_NOPE + MLA_ROPE)
    w = jnp.concatenate([w, jnp.zeros((r, N_HEADS, 2 * LANE - MLA_NOPE - MLA_ROPE), w_uq.dtype)], axis=-1)
    return w.reshape(r, N_HEADS * 2 * LANE).astype(CDT)


def kernel(x, p, w_in, b_f, g_cq, g_ckv, w_uq, w_ukv, rel_bias, w_branch, w_gate, b_gate, w_out, w1_gate, w1_up, w1_down, w2_gate, w2_up, w2_down, g_ffn1_pre, g_ffn1_post, g_mix_pre, g_mix_post, g_ffn2_pre, g_ffn2_post, g_ple_pre, g_ple_post, w_ple, w_ple_gate):
    batch, seq, d = x.shape
    depth = w_in.shape[0]
    t = batch * seq
    cos128, sin128 = _rope_tables(seq, HEAD_DIM // 2)
    cos64, sin64 = _rope_tables(seq, IDX_DIM // 2)

    w1_gate, w1_up, w1_down, w2_gate, w2_up, w2_down, w_gate, w_branch, w_out, w_ple_gate = (
        w.astype(CDT) for w in (w1_gate, w1_up, w1_down, w2_gate, w2_up, w2_down, w_gate, w_branch, w_out, w_ple_gate))

    h = x.reshape(t, d)
    u = rmsnorm_call(h, g_ffn1_pre[0])
    for i in range(depth):
        a = swiglu_call(u, w1_gate, w1_up, i)
        h, u = rowres_call(a, w1_down, i, h, g_ffn1_post[i], g_mix_pre[i], 0.5)

        w1, w2, w3, w4, w5, w5t = _pack_w_in(w_in[i])
        z1 = mm_call(u, w1)
        z2 = proj_rope_call(u, w2, cos128, sin128, seq, HEAD_DIM // 2, range(HB + 1))
        z3 = proj_rope_call(u, w3, cos64, sin64, seq, IDX_DIM // 2, range(HB + 3))
        qm, kvb = proj_mla_call(u, w4, g_cq[i], g_ckv[i], _pack_w_uq(w_uq[i]), w_ukv[i].astype(CDT), cos64, sin64, seq)
        tail, tailt = proj_tail_call(u, w5, w5t, batch, seq)
        drow = decay_call(tailt, b_f[i], batch, seq)
        o_all = fox_call(z1, drow, batch, seq)
        o_all = mla_call(qm, kvb, z3, HB + 2, o_all, batch, seq)
        o_all = band_call(z1, band_bias_call(rel_bias[i]), o_all, batch, seq)
        o_all = dsa_call(z2, z3, tail, o_all, batch, seq)
        merged = merge_call(u, o_all, w_gate, b_gate[i].reshape(-1, 1, d), w_branch, i)
        h, u = rowres_call(merged, w_out, i, h, g_mix_post[i], g_ffn2_pre[i], 1.0)

        a = swiglu_call(u, w2_gate, w2_up, i)
        h, u = rowres_call(a, w2_down, i, h, g_ffn2_post[i], g_ple_pre[i], 0.5)

        last = i == depth - 1
        g_next = g_ple_pre[i] if last else g_ffn1_pre[i + 1]
        h, u = rowres_call(u, w_ple_gate, i, h, g_ple_post[i], g_next, 1.0,
                           ple=(p[i].reshape(t, -1), w_ple[i].astype(CDT)), emit_u=not last)
    return h.reshape(batch, seq, d)
```
